```python
import jax, jax.numpy as jnp
from jax import lax
import numpy as np

D_MODEL = 1024
BATCH = 1
SEQ = 16384
DEPTH = 1
DEC_BATCH = 32
DEC_SEQ = 2048
PAST_LEN = 128

PLE_DIM = 256
D_FF = 2816
ROPE_THETA = 500000.0
NORM_EPS = 1e-6
NEG_INF = -1e30
MLA_HEADS = 8
Q_LORA = 512
KV_LORA = 256
QK_NOPE = 128
QK_ROPE = 64
QK_HEAD = QK_NOPE + QK_ROPE
V_HEAD = 128
MLA_QBLK = 128
DIL_PATTERN = ((128, 1), (512, 4), (2048, 16))
N_GROUPS = 3
DIL_HEADS = 8
DIL_HEAD = 64
DIL_ROT = DIL_HEAD // 4
DIL_BLK = 64
DIL_QKV = 3 * N_GROUPS * DIL_HEADS * DIL_HEAD
N_BRANCH = 2
IN_SPLITS = (Q_LORA, Q_LORA + KV_LORA, Q_LORA + KV_LORA + QK_ROPE, Q_LORA + KV_LORA + QK_ROPE + DIL_QKV)
D_IN = Q_LORA + KV_LORA + QK_ROPE + DIL_QKV + N_BRANCH * D_MODEL

kernel_name = 'hybrid_mla_dilated_macaron_encoder'


def rmsnorm(x, g):
    xf = x.astype(jnp.float32)
    y = xf * lax.rsqrt(jnp.mean(xf * xf, axis=-1, keepdims=True) + NORM_EPS)
    return (y * g.astype(jnp.float32)).astype(x.dtype)


def swiglu(x, w_gate, w_up, w_down):
    return (jax.nn.silu(x @ w_gate) * (x @ w_up)) @ w_down


def rotary(x):
    s, r = x.shape[1], x.shape[-1]
    inv = ROPE_THETA ** (-jnp.arange(0, r, 2, dtype=jnp.float32) / r)
    ang = jnp.arange(s, dtype=jnp.float32)[:, None] * inv[None, :]
    cos = jnp.cos(ang)[None, :, None, :]
    sin = jnp.sin(ang)[None, :, None, :]
    x1, x2 = jnp.split(x.astype(jnp.float32), 2, axis=-1)
    return jnp.concatenate([x1 * cos - x2 * sin, x2 * cos + x1 * sin], axis=-1).astype(x.dtype)


def mla_attention(q, k, v):
    b, s, h, dq = q.shape
    nq = s // MLA_QBLK
    qb = q.reshape(b, nq, MLA_QBLK, h, dq).transpose(1, 0, 2, 3, 4)
    scale = dq ** -0.5

    def one_block(qblk):
        sc = jnp.einsum('bqhd,bkhd->bhqk', qblk, k).astype(jnp.float32) * scale
        pr = jax.nn.softmax(sc, axis=-1)
        return jnp.einsum('bhqk,bkhd->bqhd', pr.astype(v.dtype), v)

    o = lax.map(one_block, qb)
    return o.transpose(1, 0, 2, 3, 4).reshape(b, s, h, v.shape[-1])


def dilated_group(q, k, v, dilation, n_side):
    b, s, h, dh = q.shape
    L = s // dilation
    nb = -(-L // DIL_BLK)
    lp = nb * DIL_BLK

    def sub(t):
        return t.reshape(b, L, dilation, h, dh).transpose(0, 2, 1, 3, 4)

    qs = jnp.pad(sub(q), ((0, 0), (0, 0), (0, lp - L), (0, 0), (0, 0))).reshape(b, dilation, nb, DIL_BLK, h, dh)

    def band(t):
        tp = jnp.pad(sub(t), ((0, 0), (0, 0), (DIL_BLK, lp - L + DIL_BLK), (0, 0), (0, 0)))
        tp = tp.reshape(b, dilation, nb + 2, DIL_BLK, h, dh)
        return jnp.concatenate([tp[:, :, :-2], tp[:, :, 1:-1], tp[:, :, 2:]], axis=3)

    kw, vw = band(k), band(v)
    qi = jnp.arange(DIL_BLK)[:, None]
    kt = jnp.arange(3 * DIL_BLK)[None, :]
    rel = kt - DIL_BLK - qi
    jk = jnp.arange(nb)[:, None, None] * DIL_BLK + (kt - DIL_BLK)[None]
    valid = (jnp.abs(rel) <= n_side)[None] & (jk >= 0) & (jk < L)

    sc = jnp.einsum('brnqhd,brnkhd->brnhqk', qs, kw).astype(jnp.float32) * (dh ** -0.5)
    sc = jnp.where(valid[None, None, :, None], sc, NEG_INF)
    lse = jax.nn.logsumexp(sc, axis=-1)
    pr = jnp.exp(sc - lse[..., None])
    o = jnp.einsum('brnhqk,brnkhd->brnqhd', pr.astype(v.dtype), vw)
    o = o.reshape(b, dilation, lp, h, dh)[:, :, :L].transpose(0, 2, 1, 3, 4).reshape(b, s, h, dh)
    lse = lse.transpose(0, 1, 2, 4, 3).reshape(b, dilation, lp, h)[:, :, :L].transpose(0, 2, 1, 3).reshape(b, s, h)
    return o, lse


def partial_rotary(x):
    return jnp.concatenate([rotary(x[..., :DIL_ROT]), x[..., DIL_ROT:]], axis=-1)


def encoder_layer(x, p, ffn1_norm, ffn1_w_gate, ffn1_w_up, ffn1_w_down, mix_norm, w_in,
                  q_a_norm, w_uq, kv_a_norm, w_ukv, mla_q_norm, mla_k_norm, dil_q_norm, dil_k_norm,
                  w_o_mla, w_o_dil, w_out, ffn2_norm, ffn2_w_gate, ffn2_w_up, ffn2_w_down,
                  ple_norm, w_ple, w_ple_gate):
    b, s, _ = x.shape
    x = x + 0.5 * swiglu(rmsnorm(x, ffn1_norm), ffn1_w_gate, ffn1_w_up, ffn1_w_down)

    u = rmsnorm(x, mix_norm)
    c_q, c_kv, k_pe, dil, gate_logits = jnp.split(u @ w_in, IN_SPLITS, axis=-1)

    q = (rmsnorm(c_q, q_a_norm) @ w_uq).reshape(b, s, MLA_HEADS, QK_HEAD)
    kv = (rmsnorm(c_kv, kv_a_norm) @ w_ukv).reshape(b, s, MLA_HEADS, QK_NOPE + V_HEAD)
    k_nope, v_mla = kv[..., :QK_NOPE], kv[..., QK_NOPE:]
    k = jnp.concatenate([k_nope, jnp.broadcast_to(k_pe[:, :, None, :], (b, s, MLA_HEADS, QK_ROPE))], axis=-1)
    q = rmsnorm(q, mla_q_norm)
    k = rmsnorm(k, mla_k_norm)
    q = jnp.concatenate([q[..., :QK_NOPE], rotary(q[..., QK_NOPE:])], axis=-1)
    k = jnp.concatenate([k[..., :QK_NOPE], rotary(k[..., QK_NOPE:])], axis=-1)
    o_mla = mla_attention(q, k, v_mla).reshape(b, s, MLA_HEADS * V_HEAD)

    dil = dil.reshape(b, s, N_GROUPS, 3, DIL_HEADS, DIL_HEAD)
    outs, lses = [], []
    for g, (window, dilation) in enumerate(DIL_PATTERN):
        qg = partial_rotary(rmsnorm(dil[:, :, g, 0], dil_q_norm[g]))
        kg = partial_rotary(rmsnorm(dil[:, :, g, 1], dil_k_norm[g]))
        og, lg = dilated_group(qg, kg, dil[:, :, g, 2], dilation, window // (2 * dilation))
        outs.append(og)
        lses.append(lg)
    w_grp = jax.nn.softmax(jnp.stack(lses, axis=0), axis=0)
    o_dil = jnp.einsum('gbsh,gbshd->bshd', w_grp, jnp.stack(outs, axis=0).astype(jnp.float32))
    o_dil = o_dil.astype(x.dtype).reshape(b, s, DIL_HEADS * DIL_HEAD)

    gates = jax.nn.sigmoid(gate_logits)
    g_a, g_b = gates[..., :D_MODEL], gates[..., D_MODEL:]
    x = x + (g_a * (o_mla @ w_o_mla) + g_b * (o_dil @ w_o_dil)) @ w_out

    x = x + 0.5 * swiglu(rmsnorm(x, ffn2_norm), ffn2_w_gate, ffn2_w_up, ffn2_w_down)

    x = x + jax.nn.sigmoid(rmsnorm(x, ple_norm) @ w_ple_gate) * (p @ w_ple)
    return x


def setup_inputs(seed: int = 0) -> dict:
    key = jax.random.key(seed)
    ks = jax.random.split(key, 32)
    f32 = jnp.float32

    def nrm(k, shape, scale):
        return jax.random.normal(k, shape, f32) * scale

    def gain(k, shape):
        return 1.0 + 0.02 * jax.random.normal(k, shape, f32)

    L = DEPTH
    return {
        'x_prompt': nrm(ks[0], (BATCH, SEQ, D_MODEL), 1.0),
        'x_sample': nrm(ks[1], (DEC_BATCH, DEC_SEQ, D_MODEL), 1.0),
        'p_prompt': nrm(ks[2], (DEPTH, BATCH, SEQ, PLE_DIM), 1.0),
        'p_sample': nrm(ks[3], (DEPTH, DEC_BATCH, DEC_SEQ, PLE_DIM), 1.0),
        'ffn1_norm': gain(ks[4], (L, D_MODEL)),
        'ffn1_w_gate': nrm(ks[5], (L, D_MODEL, D_FF), D_MODEL ** -0.5),
        'ffn1_w_up': nrm(ks[6], (L, D_MODEL, D_FF), D_MODEL ** -0.5),
        'ffn1_w_down': nrm(ks[7], (L, D_FF, D_MODEL), D_FF ** -0.5),
        'mix_norm': gain(ks[8], (L, D_MODEL)),
        'w_in': nrm(ks[9], (L, D_MODEL, D_IN), D_MODEL ** -0.5),
        'q_a_norm': gain(ks[10], (L, Q_LORA)),
        'w_uq': nrm(ks[11], (L, Q_LORA, MLA_HEADS * QK_HEAD), Q_LORA ** -0.5),
        'kv_a_norm': gain(ks[12], (L, KV_LORA)),
        'w_ukv': nrm(ks[13], (L, KV_LORA, MLA_HEADS * (QK_NOPE + V_HEAD)), KV_LORA ** -0.5),
        'mla_q_norm': gain(ks[14], (L, QK_HEAD)),
        'mla_k_norm': gain(ks[15], (L, QK_HEAD)),
        'dil_q_norm': gain(ks[16], (L, N_GROUPS, DIL_HEAD)),
        'dil_k_norm': gain(ks[17], (L, N_GROUPS, DIL_HEAD)),
        'w_o_mla': nrm(ks[18], (L, MLA_HEADS * V_HEAD, D_MODEL), (MLA_HEADS * V_HEAD) ** -0.5),
        'w_o_dil': nrm(ks[19], (L, DIL_HEADS * DIL_HEAD, D_MODEL), (DIL_HEADS * DIL_HEAD) ** -0.5),
        'w_out': nrm(ks[20], (L, D_MODEL, D_MODEL), D_MODEL ** -0.5),
        'ffn2_norm': gain(ks[21], (L, D_MODEL)),
        'ffn2_w_gate': nrm(ks[22], (L, D_MODEL, D_FF), D_MODEL ** -0.5),
        'ffn2_w_up': nrm(ks[23], (L, D_MODEL, D_FF), D_MODEL ** -0.5),
        'ffn2_w_down': nrm(ks[24], (L, D_FF, D_MODEL), D_FF ** -0.5),
        'ple_norm': gain(ks[25], (L, D_MODEL)),
        'w_ple': nrm(ks[26], (L, PLE_DIM, D_MODEL), PLE_DIM ** -0.5),
        'w_ple_gate': nrm(ks[27], (L, D_MODEL, D_MODEL), D_MODEL ** -0.5),
    }


def reference(x_prompt, x_sample, p_prompt, p_sample, ffn1_norm, ffn1_w_gate, ffn1_w_up, ffn1_w_down,
              mix_norm, w_in, q_a_norm, w_uq, kv_a_norm, w_ukv, mla_q_norm, mla_k_norm,
              dil_q_norm, dil_k_norm, w_o_mla, w_o_dil, w_out, ffn2_norm, ffn2_w_gate, ffn2_w_up,
              ffn2_w_down, ple_norm, w_ple, w_ple_gate):
    weights = (ffn1_norm, ffn1_w_gate, ffn1_w_up, ffn1_w_down, mix_norm, w_in, q_a_norm, w_uq,
               kv_a_norm, w_ukv, mla_q_norm, mla_k_norm, dil_q_norm, dil_k_norm, w_o_mla, w_o_dil,
               w_out, ffn2_norm, ffn2_w_gate, ffn2_w_up, ffn2_w_down, ple_norm, w_ple, w_ple_gate)

    def run(x, p):
        for i in range(DEPTH):
            x = encoder_layer(x, p[i], *[w[i] for w in weights])
        return x

    y_prompt = run(x_prompt, p_prompt)
    y_sample = run(x_sample, p_sample)
    return (y_prompt, y_sample)
```

```python
import functools

import numpy as np
import jax
import jax.numpy as jnp
from jax import lax
from jax.experimental import pallas as pl
from jax.experimental.pallas import tpu as pltpu

F32 = jnp.float32
BF16 = jnp.bfloat16

D_MODEL = 1024
PLE_DIM = 256
D_FF = 2816
ROPE_THETA = 500000.0
NORM_EPS = 1e-6
NEG_INF = -1e30
MLA_HEADS = 8
Q_LORA = 512
KV_LORA = 256
QK_NOPE = 128
QK_ROPE = 64
QK_HEAD = QK_NOPE + QK_ROPE
V_HEAD = 128
DIL_PATTERN = ((128, 1), (512, 4), (2048, 16))
N_GROUPS = 3
DIL_HEADS = 8
DIL_HEAD = 64
DIL_ROT = DIL_HEAD // 4
DIL_WIDTH = DIL_HEADS * DIL_HEAD
DIL_QKV = 3 * N_GROUPS * DIL_WIDTH
N_SIDE = 64
assert all(w // (2 * d) == N_SIDE for w, d in DIL_PATTERN)

LANE = 128
HEAD_PAD = 2 * LANE
IN_HEAD = Q_LORA + KV_LORA + LANE
VMEM_LIMIT = 56 * 1024 * 1024

TM_FFN = 512
TM_PROJ = 256
TM_MERGE = 256
TQ_MLA = 512
TK_MLA = 512
TL_DIL = 256
FF_CHUNK = D_FF // 2


def _cparams(*sem):
    return pltpu.CompilerParams(dimension_semantics=sem, vmem_limit_bytes=VMEM_LIMIT)


def _const_spec(shape):
    nd = len(shape)
    return pl.BlockSpec(shape, lambda *_: (0,) * nd, pipeline_mode=pl.Buffered(1))


def _rms(x, g):
    return x * lax.rsqrt(jnp.mean(x * x, axis=-1, keepdims=True) + NORM_EPS) * g


def _dot(a, b):
    return jnp.dot(a, b, preferred_element_type=F32)


def _dot_nt(a, b):
    return lax.dot_general(a, b, (((1,), (1,)), ((), ())), preferred_element_type=F32)


def _ffn_body(x, g_ref, wg_ref, wu_ref, wd_ref):
    h = _rms(x, g_ref[...]).astype(BF16)
    acc = jnp.zeros_like(x)
    for c in range(D_FF // FF_CHUNK):
        cols = slice(c * FF_CHUNK, (c + 1) * FF_CHUNK)
        gate = _dot(h, wg_ref[:, cols])
        up = _dot(h, wu_ref[:, cols])
        act = (gate * jax.nn.sigmoid(gate) * up).astype(BF16)
        acc = acc + _dot(act, wd_ref[cols, :])
    return x + 0.5 * acc


def _ffn_kernel(x_ref, g_ref, wg_ref, wu_ref, wd_ref, o_ref):
    o_ref[...] = _ffn_body(x_ref[...], g_ref, wg_ref, wu_ref, wd_ref)


def _ffn_ple_kernel(x_ref, p_ref, g_ref, wg_ref, wu_ref, wd_ref, gp_ref, wpg_ref, wpe_ref, o_ref):
    y = _ffn_body(x_ref[...], g_ref, wg_ref, wu_ref, wd_ref)
    gate = jax.nn.sigmoid(_dot(_rms(y, gp_ref[...]).astype(BF16), wpg_ref[...]))
    o_ref[...] = y + gate * _dot(p_ref[...].astype(BF16), wpe_ref[...])


def _ffn(x, w, prefix, p=None):
    t = x.shape[0]
    tm = min(TM_FFN, t)
    row = pl.BlockSpec((tm, D_MODEL), lambda i: (i, 0))
    ffn_w = [w[prefix + '_norm'], w[prefix + '_w_gate'], w[prefix + '_w_up'], w[prefix + '_w_down']]
    if p is None:
        kern, args, specs = _ffn_kernel, [x] + ffn_w, [row]
    else:
        kern, args = _ffn_ple_kernel, [x, p] + ffn_w + [w['ple_norm'], w['w_ple_gate'], w['w_ple']]
        specs = [row, pl.BlockSpec((tm, PLE_DIM), lambda i: (i, 0))]
    specs = specs + [_const_spec(a.shape) for a in args[len(specs):]]
    return pl.pallas_call(
        kern, grid=(t // tm,), in_specs=specs, out_specs=row,
        out_shape=jax.ShapeDtypeStruct((t, D_MODEL), F32),
        compiler_params=_cparams("parallel"), name=prefix)(*args)


def _rope64(y, cos, sin):
    return y * cos + pltpu.roll(y, 64, 1) * sin


def _first_head_qk_lanes(rows):
    lane = lax.broadcasted_iota(jnp.int32, (rows, LANE), 1)
    return jnp.where(lane < 8, 1, jnp.where(lane < 16, 0, jnp.where(lane < 72, 1, 0))) > 0


def _inproj_kernel(x_ref, gmix_ref, win_ref, gqa_ref, wuq_ref, gkva_ref, wuk_ref, wuv_ref,
                   gq_ref, gk_ref, gdq_ref, gdk_ref, cm_ref, sm_ref, cd_ref, sd_ref,
                   q_ref, k_ref, v_ref, dil_ref):
    u = _rms(x_ref[...], gmix_ref[...]).astype(BF16)
    head = _dot(u, win_ref[:, :IN_HEAD])
    c_q = _rms(head[:, :Q_LORA], gqa_ref[...]).astype(BF16)
    c_kv = _rms(head[:, Q_LORA:Q_LORA + KV_LORA], gkva_ref[...]).astype(BF16)
    k_pe = head[:, Q_LORA + KV_LORA:]
    q_raw = _dot(c_q, wuq_ref[...])
    k_nope = _dot(c_kv, wuk_ref[...])
    v_ref[...] = _dot(c_kv, wuv_ref[...]).astype(BF16)

    cm, sm = cm_ref[...], sm_ref[...]
    gq, gk = gq_ref[...], gk_ref[...]
    k_rot = _rope64(k_pe * gk[:, LANE:], cm, sm)
    k_pe_ss = jnp.sum(k_pe * k_pe, axis=-1, keepdims=True)
    q_scale = QK_HEAD ** -0.5
    for h in range(MLA_HEADS):
        lo, mid, hi = h * HEAD_PAD, h * HEAD_PAD + LANE, (h + 1) * HEAD_PAD
        q0, q1 = q_raw[:, lo:mid], q_raw[:, mid:hi]
        ss = jnp.sum(q0 * q0 + q1 * q1, axis=-1, keepdims=True)
        r = lax.rsqrt(ss * (1.0 / QK_HEAD) + NORM_EPS) * q_scale
        q_ref[:, lo:mid] = (q0 * r * gq[:, :LANE]).astype(BF16)
        q_ref[:, mid:hi] = (_rope64(q1 * gq[:, LANE:], cm, sm) * r).astype(BF16)
        kn = k_nope[:, h * LANE:(h + 1) * LANE]
        ssk = jnp.sum(kn * kn, axis=-1, keepdims=True) + k_pe_ss
        rk = lax.rsqrt(ssk * (1.0 / QK_HEAD) + NORM_EPS)
        k_ref[:, lo:mid] = (kn * rk * gk[:, :LANE]).astype(BF16)
        k_ref[:, mid:hi] = (k_rot * rk).astype(BF16)

    cd, sd = cd_ref[...], sd_ref[...]
    in_a = _first_head_qk_lanes(u.shape[0])
    group_cols = 3 * DIL_WIDTH
    for g in range(N_GROUPS):
        base = g * group_cols
        sec = _dot(u, win_ref[:, IN_HEAD + base:IN_HEAD + base + group_cols])
        for j, (gain_ref, scale) in enumerate(((gdq_ref, DIL_HEAD ** -0.5), (gdk_ref, 1.0))):
            gain = gain_ref[g:g + 1, :]
            for c in range(DIL_WIDTH // LANE):
                cols = slice(j * DIL_WIDTH + c * LANE, j * DIL_WIDTH + (c + 1) * LANE)
                xx = sec[:, cols]
                sq = xx * xx
                ss_a = jnp.sum(jnp.where(in_a, sq, 0.0), axis=-1, keepdims=True)
                ss_b = jnp.sum(jnp.where(in_a, 0.0, sq), axis=-1, keepdims=True)
                r_a = lax.rsqrt(ss_a * (1.0 / DIL_HEAD) + NORM_EPS)
                r_b = lax.rsqrt(ss_b * (1.0 / DIL_HEAD) + NORM_EPS)
                out = _rope64(xx * gain, cd, sd) * (jnp.where(in_a, r_a, r_b) * scale)
                dil_ref[:, base + cols.start:base + cols.stop] = out.astype(BF16)
        dil_ref[:, base + 2 * DIL_WIDTH:base + group_cols] = sec[:, 2 * DIL_WIDTH:].astype(BF16)


def _inproj(x1, w, tabs, seq):
    t = x1.shape[0]
    tm = min(TM_PROJ, seq)
    nper = seq // tm
    row = lambda n: pl.BlockSpec((tm, n), lambda i: (i, 0))
    tab = pl.BlockSpec((tm, LANE), lambda i: (i % nper, 0))
    consts = [w['mix_norm'], w['w_in_main'], w['q_a_norm'], w['w_uq'], w['kv_a_norm'], w['w_uk'], w['w_uv'],
              w['mla_q_gain'], w['mla_k_gain'], w['dil_q_gain'], w['dil_k_gain']]
    widths = (MLA_HEADS * HEAD_PAD, MLA_HEADS * HEAD_PAD, MLA_HEADS * V_HEAD, DIL_QKV)
    return pl.pallas_call(
        _inproj_kernel, grid=(t // tm,),
        in_specs=[row(D_MODEL)] + [_const_spec(a.shape) for a in consts] + [tab] * 4,
        out_specs=[row(n) for n in widths],
        out_shape=[jax.ShapeDtypeStruct((t, n), BF16) for n in widths],
        compiler_params=_cparams("parallel"), name="inproj")(x1, *consts, *tabs)


def _mla_kernel(q_ref, k_ref, v_ref, o_ref, *, tk, nk):
    q = q_ref[...]
    tq = q.shape[0]

    def step(j, carry):
        m, l, acc = carry
        start = pl.multiple_of(j * tk, tk)
        s = _dot_nt(q, k_ref[pl.ds(start, tk), :])
        m_new = jnp.maximum(m, jnp.max(s, axis=-1, keepdims=True))
        alpha = jnp.exp(m - m_new)
        p = jnp.exp(s - m_new)
        l = alpha * l + jnp.sum(p, axis=-1, keepdims=True)
        acc = alpha * acc + _dot(p.astype(BF16), v_ref[pl.ds(start, tk), :])
        return m_new, l, acc

    init = (jnp.full((tq, 1), -jnp.inf, F32), jnp.zeros((tq, 1), F32), jnp.zeros((tq, V_HEAD), F32))
    _, l, acc = lax.fori_loop(0, nk, step, init)
    o_ref[...] = (acc / l).astype(BF16)


def _mla(q, k, v, batch, seq):
    t = q.shape[0]
    tq, tk = min(TQ_MLA, seq), min(TK_MLA, seq)
    nq = seq // tq
    return pl.pallas_call(
        functools.partial(_mla_kernel, tk=tk, nk=seq // tk),
        grid=(batch, MLA_HEADS, nq),
        in_specs=[pl.BlockSpec((tq, HEAD_PAD), lambda b, h, i: (b * nq + i, h)),
                  pl.BlockSpec((seq, HEAD_PAD), lambda b, h, i: (b, h)),
                  pl.BlockSpec((seq, V_HEAD), lambda b, h, i: (b, h))],
        out_specs=pl.BlockSpec((tq, V_HEAD), lambda b, h, i: (b * nq + i, h)),
        out_shape=jax.ShapeDtypeStruct((t, MLA_HEADS * V_HEAD), BF16),
        compiler_params=_cparams("parallel", "parallel", "arbitrary"), name="mla")(q, k, v)


def _dil_kernel(q_ref, kp_ref, kc_ref, kn_ref, vp_ref, vc_ref, vn_ref, o_ref, lse_ref, *, tl, sub_len):
    i = pl.program_id(2)
    tk = tl + 2 * N_SIDE
    q = q_ref[0]
    k = jnp.concatenate([kp_ref[0], kc_ref[0], kn_ref[0]], axis=0)
    v = jnp.concatenate([vp_ref[0], vc_ref[0], vn_ref[0]], axis=0)
    q_idx = i * tl + lax.broadcasted_iota(jnp.int32, (tl, 1), 0)
    k_idx = i * tl - N_SIDE + lax.broadcasted_iota(jnp.int32, (1, tk), 1)
    lo = jnp.maximum(q_idx - N_SIDE, 0)
    hi = jnp.minimum(q_idx + N_SIDE, sub_len - 1)
    bias = jnp.where(k_idx >= lo, jnp.where(k_idx <= hi, 0.0, NEG_INF), NEG_INF)
    in_a = _first_head_qk_lanes(tl)
    first = lax.broadcasted_iota(jnp.int32, (tl, LANE), 1) < DIL_HEAD
    zero = jnp.zeros((), BF16)
    for c in range(DIL_WIDTH // LANE):
        cols = slice(c * LANE, (c + 1) * LANE)
        qp, kpair, vpair = q[:, cols], k[:, cols], v[:, cols]
        outs, lses = [], []
        for q_head in (jnp.where(in_a, qp, zero), jnp.where(in_a, zero, qp)):
            s = _dot_nt(q_head, kpair) + bias
            m = jnp.max(s, axis=-1, keepdims=True)
            p = jnp.exp(s - m)
            l = jnp.sum(p, axis=-1, keepdims=True)
            outs.append(_dot(p.astype(BF16), vpair) / l)
            lses.append(m + jnp.log(l))
        o_ref[0, :, cols] = jnp.where(first, outs[0], outs[1])
        lse_ref[0, :, cols] = jnp.where(first, lses[0], lses[1])


def _dilated_group(dil, g, dilation, batch, seq):
    sub_len = seq // dilation
    tl = min(TL_DIL, sub_len)
    nblk = sub_len // tl
    halo_per_blk = tl // N_SIDE
    n_halo = sub_len // N_SIDE
    nparts = DIL_QKV // DIL_WIDTH
    view = dil.reshape(batch, sub_len, dilation * DIL_QKV)

    def spec(rows, part, blk):
        return pl.BlockSpec((1, rows, DIL_WIDTH), lambda b, r, i: (b, blk(i), r * nparts + 3 * g + part))

    cur = lambda i: i
    prev = lambda i: jnp.maximum(i * halo_per_blk - 1, 0)
    nxt = lambda i: jnp.minimum((i + 1) * halo_per_blk, n_halo - 1)
    out_spec = pl.BlockSpec((1, tl, DIL_WIDTH), lambda b, r, i: (b, i, r))
    out_sds = jax.ShapeDtypeStruct((batch, sub_len, dilation * DIL_WIDTH), F32)
    o, lse = pl.pallas_call(
        functools.partial(_dil_kernel, tl=tl, sub_len=sub_len),
        grid=(batch, dilation, nblk),
        in_specs=[spec(tl, 0, cur),
                  spec(N_SIDE, 1, prev), spec(tl, 1, cur), spec(N_SIDE, 1, nxt),
                  spec(N_SIDE, 2, prev), spec(tl, 2, cur), spec(N_SIDE, 2, nxt)],
        out_specs=[out_spec, out_spec], out_shape=[out_sds, out_sds],
        compiler_params=_cparams("parallel", "parallel", "arbitrary"), name=f"dilated{g}")(*([view] * 7))
    return o.reshape(batch * seq, DIL_WIDTH), lse.reshape(batch * seq, DIL_WIDTH)


def _merge_kernel(x_ref, om_ref, o0_ref, o1_ref, o2_ref, l0_ref, l1_ref, l2_ref,
                  gmix_ref, wgate_ref, woa_ref, wob_ref, wout_ref, y_ref):
    x = x_ref[...]
    lses = [l0_ref[...], l1_ref[...], l2_ref[...]]
    top = jnp.maximum(jnp.maximum(lses[0], lses[1]), lses[2])
    e = [jnp.exp(l - top) for l in lses]
    o_dil = (e[0] * o0_ref[...] + e[1] * o1_ref[...] + e[2] * o2_ref[...]) / (e[0] + e[1] + e[2])
    u = _rms(x, gmix_ref[...]).astype(BF16)
    gates = jax.nn.sigmoid(_dot(u, wgate_ref[...]))
    a = _dot(om_ref[...], woa_ref[...])
    b = _dot(o_dil.astype(BF16), wob_ref[...])
    mix = gates[:, :D_MODEL] * a + gates[:, D_MODEL:] * b
    y_ref[...] = x + _dot(mix.astype(BF16), wout_ref[...])


def _merge(x1, o_mla, dil_outs, w):
    t = x1.shape[0]
    tm = min(TM_MERGE, t)
    row = lambda n: pl.BlockSpec((tm, n), lambda i: (i, 0))
    outs = [o for o, _ in dil_outs]
    lses = [l for _, l in dil_outs]
    consts = [w['mix_norm'], w['w_gates'], w['w_o_mla'], w['w_o_dil'], w['w_out']]
    return pl.pallas_call(
        _merge_kernel, grid=(t // tm,),
        in_specs=[row(D_MODEL), row(MLA_HEADS * V_HEAD)] + [row(DIL_WIDTH)] * 6
                 + [_const_spec(a.shape) for a in consts],
        out_specs=row(D_MODEL), out_shape=jax.ShapeDtypeStruct((t, D_MODEL), F32),
        compiler_params=_cparams("parallel"), name="merge")(x1, o_mla, *outs, *lses, *consts)


def _pair_perm():
    r = DIL_ROT // 2
    a, b = np.arange(DIL_HEAD), DIL_HEAD + np.arange(DIL_HEAD)
    return np.concatenate([a[:r], b[:r], a[2 * r:], a[r:2 * r], b[r:2 * r], b[2 * r:]])


def _rope_pad_cols():
    half = QK_ROPE // 2
    cols = -np.ones(LANE, np.int64)
    cols[:half] = np.arange(half)
    cols[LANE // 2:LANE // 2 + half] = half + np.arange(half)
    return cols


def _take_cols(a, cols):
    cols = np.asarray(cols)
    out = jnp.take(a, jnp.asarray(np.maximum(cols, 0)), axis=-1)
    return jnp.where(jnp.asarray(cols >= 0), out, 0)


def _prep_weights(w):
    pad = _rope_pad_cols()
    perm = _pair_perm()
    out = dict(w)
    for name in ('ffn1_norm', 'mix_norm', 'q_a_norm', 'kv_a_norm', 'ffn2_norm', 'ple_norm'):
        out[name] = w[name].reshape(1, -1)
    for name in ('ffn1_w_gate', 'ffn1_w_up', 'ffn1_w_down', 'ffn2_w_gate', 'ffn2_w_up', 'ffn2_w_down',
                 'w_o_mla', 'w_o_dil', 'w_out', 'w_ple', 'w_ple_gate'):
        out[name] = w[name].astype(BF16)

    w_in = w['w_in']
    o_kpe, o_dil, o_gate = Q_LORA + KV_LORA, Q_LORA + KV_LORA + QK_ROPE, Q_LORA + KV_LORA + QK_ROPE + DIL_QKV
    dil_cols = []
    for g in range(N_GROUPS):
        for part in range(3):
            for c in range(DIL_WIDTH // LANE):
                base = o_dil + (3 * g + part) * DIL_WIDTH + c * LANE
                dil_cols.append(base + (perm if part < 2 else np.arange(LANE)))
    main_cols = np.concatenate([np.arange(o_kpe), np.where(pad >= 0, o_kpe + pad, -1)] + dil_cols)
    out['w_in_main'] = _take_cols(w_in, main_cols).astype(BF16)
    out['w_gates'] = w_in[:, o_gate:].astype(BF16)

    head_cols = np.concatenate([np.arange(QK_NOPE), np.where(pad >= 0, QK_NOPE + pad, -1)])
    uq_cols = np.concatenate([np.where(head_cols >= 0, h * QK_HEAD + head_cols, -1) for h in range(MLA_HEADS)])
    out['w_uq'] = _take_cols(w['w_uq'], uq_cols).astype(BF16)
    ukv = w['w_ukv'].reshape(KV_LORA, MLA_HEADS, QK_NOPE + V_HEAD)
    out['w_uk'] = ukv[:, :, :QK_NOPE].reshape(KV_LORA, -1).astype(BF16)
    out['w_uv'] = ukv[:, :, QK_NOPE:].reshape(KV_LORA, -1).astype(BF16)
    out['mla_q_gain'] = _take_cols(w['mla_q_norm'], head_cols).reshape(1, -1)
    out['mla_k_gain'] = _take_cols(w['mla_k_norm'], head_cols).reshape(1, -1)
    out['dil_q_gain'] = _take_cols(w['dil_q_norm'], perm % DIL_HEAD)
    out['dil_k_gain'] = _take_cols(w['dil_k_norm'], perm % DIL_HEAD)
    return out


def _rope_tables(seq):
    pos = jnp.arange(seq, dtype=F32)[:, None]

    def angles(r):
        inv = ROPE_THETA ** (-jnp.arange(0, r, 2, dtype=F32) / r)
        ang = pos * inv[None, :]
        return jnp.cos(ang), jnp.sin(ang)

    cos, sin = angles(QK_ROPE)
    half = QK_ROPE // 2
    zeros = jnp.zeros((seq, LANE // 2 - half), F32)
    cm = jnp.concatenate([cos, zeros, cos, zeros], axis=1)
    sm = jnp.concatenate([-sin, zeros, sin, zeros], axis=1)

    cos, sin = angles(DIL_ROT)
    rest = LANE // 2 - DIL_ROT
    cd = jnp.concatenate([cos, cos, jnp.ones((seq, rest), F32)] * 2, axis=1)
    sd = jnp.concatenate([-sin, -sin, jnp.zeros((seq, rest), F32), sin, sin, jnp.zeros((seq, rest), F32)], axis=1)
    return cm, sm, cd, sd


def _encoder_layer(x, p, w):
    batch, seq, _ = x.shape
    x = x.reshape(batch * seq, D_MODEL)
    p = p.reshape(batch * seq, PLE_DIM)
    x1 = _ffn(x, w, 'ffn1')
    q, k, v, dil = _inproj(x1, w, _rope_tables(seq), seq)
    o_mla = _mla(q, k, v, batch, seq)
    dil_outs = [_dilated_group(dil, g, d, batch, seq) for g, (_, d) in enumerate(DIL_PATTERN)]
    x2 = _merge(x1, o_mla, dil_outs, w)
    return _ffn(x2, w, 'ffn2', p).reshape(batch, seq, D_MODEL)


def kernel(x_prompt, x_sample, p_prompt, p_sample, ffn1_norm, ffn1_w_gate, ffn1_w_up, ffn1_w_down, mix_norm, w_in, q_a_norm, w_uq, kv_a_norm, w_ukv, mla_q_norm, mla_k_norm, dil_q_norm, dil_k_norm, w_o_mla, w_o_dil, w_out, ffn2_norm, ffn2_w_gate, ffn2_w_up, ffn2_w_down, ple_norm, w_ple, w_ple_gate):
    stacked = dict(ffn1_norm=ffn1_norm, ffn1_w_gate=ffn1_w_gate, ffn1_w_up=ffn1_w_up, ffn1_w_down=ffn1_w_down,
                   mix_norm=mix_norm, w_in=w_in, q_a_norm=q_a_norm, w_uq=w_uq, kv_a_norm=kv_a_norm, w_ukv=w_ukv,
                   mla_q_norm=mla_q_norm, mla_k_norm=mla_k_norm, dil_q_norm=dil_q_norm, dil_k_norm=dil_k_norm,
                   w_o_mla=w_o_mla, w_o_dil=w_o_dil, w_out=w_out, ffn2_norm=ffn2_norm, ffn2_w_gate=ffn2_w_gate,
                   ffn2_w_up=ffn2_w_up, ffn2_w_down=ffn2_w_down, ple_norm=ple_norm, w_ple=w_ple,
                   w_ple_gate=w_ple_gate)
    depth = ffn1_norm.shape[0]
    layers = [_prep_weights({name: a[i] for name, a in stacked.items()}) for i in range(depth)]

    def run(x, p):
        for i in range(depth):
            x = _encoder_layer(x, p[i], layers[i])
        return x

    return run(x_prompt, p_prompt), run(x_sample, p_sample)
```

```python
import functools

import numpy as np
import jax
import jax.numpy as jnp
from jax import lax
from jax.experimental import pallas as pl
from jax.experimental.pallas import tpu as pltpu

F32 = jnp.float32
BF16 = jnp.bfloat16

D_MODEL = 1024
PLE_DIM = 256
D_FF = 2816
ROPE_THETA = 500000.0
NORM_EPS = 1e-6
NEG_INF = -1e30
MLA_HEADS = 8
Q_LORA = 512
KV_LORA = 256
QK_NOPE = 128
QK_ROPE = 64
QK_HEAD = QK_NOPE + QK_ROPE
V_HEAD = 128
DIL_PATTERN = ((128, 1), (512, 4), (2048, 16))
N_GROUPS = 3
DIL_HEADS = 8
DIL_HEAD = 64
DIL_ROT = DIL_HEAD // 4
DIL_WIDTH = DIL_HEADS * DIL_HEAD
DIL_QKV = 3 * N_GROUPS * DIL_WIDTH
N_SIDE = 64
assert all(w // (2 * d) == N_SIDE for w, d in DIL_PATTERN)

LOG2_E = 1.4426950408889634
LANE = 128
HEAD_PAD = 2 * LANE
IN_HEAD = Q_LORA + KV_LORA + LANE
VMEM_LIMIT = 56 * 1024 * 1024

TM_FFN = 512
TM_PROJ = 256
TM_MERGE = 256
TQ_MLA = 512
TK_MLA = 512
TL_DIL = 256
FF_CHUNK = D_FF // 2
MLA_UNROLL = 4


def _cparams(*sem):
    return pltpu.CompilerParams(dimension_semantics=sem, vmem_limit_bytes=VMEM_LIMIT)


def _const_spec(shape):
    nd = len(shape)
    return pl.BlockSpec(shape, lambda *_: (0,) * nd, pipeline_mode=pl.Buffered(1))


def _rms(x, g):
    return x * lax.rsqrt(jnp.mean(x * x, axis=-1, keepdims=True) + NORM_EPS) * g


def _dot(a, b):
    return jnp.dot(a, b, preferred_element_type=F32)


def _dot_nt(a, b):
    return lax.dot_general(a, b, (((1,), (1,)), ((), ())), preferred_element_type=F32)


def _ffn_body(x, g_ref, wg_ref, wu_ref, wd_ref):
    h = _rms(x, g_ref[...]).astype(BF16)
    acc = jnp.zeros_like(x)
    for c in range(D_FF // FF_CHUNK):
        cols = slice(c * FF_CHUNK, (c + 1) * FF_CHUNK)
        gate = _dot(h, wg_ref[:, cols])
        up = _dot(h, wu_ref[:, cols])
        act = (gate * jax.nn.sigmoid(gate) * up).astype(BF16)
        acc = acc + _dot(act, wd_ref[cols, :])
    return x + 0.5 * acc


def _ffn_kernel(x_ref, g_ref, wg_ref, wu_ref, wd_ref, o_ref):
    o_ref[...] = _ffn_body(x_ref[...], g_ref, wg_ref, wu_ref, wd_ref)


def _ffn_ple_kernel(x_ref, p_ref, g_ref, wg_ref, wu_ref, wd_ref, gp_ref, wpg_ref, wpe_ref, o_ref):
    y = _ffn_body(x_ref[...], g_ref, wg_ref, wu_ref, wd_ref)
    gate = jax.nn.sigmoid(_dot(_rms(y, gp_ref[...]).astype(BF16), wpg_ref[...]))
    o_ref[...] = y + gate * _dot(p_ref[...].astype(BF16), wpe_ref[...])


def _ffn(x, w, prefix, p=None):
    t = x.shape[0]
    tm = min(TM_FFN, t)
    row = pl.BlockSpec((tm, D_MODEL), lambda i: (i, 0))
    ffn_w = [w[prefix + '_norm'], w[prefix + '_w_gate'], w[prefix + '_w_up'], w[prefix + '_w_down']]
    if p is None:
        kern, args, specs = _ffn_kernel, [x] + ffn_w, [row]
    else:
        kern, args = _ffn_ple_kernel, [x, p] + ffn_w + [w['ple_norm'], w['w_ple_gate'], w['w_ple']]
        specs = [row, pl.BlockSpec((tm, PLE_DIM), lambda i: (i, 0))]
    specs = specs + [_const_spec(a.shape) for a in args[len(specs):]]
    return pl.pallas_call(
        kern, grid=(t // tm,), in_specs=specs, out_specs=row,
        out_shape=jax.ShapeDtypeStruct((t, D_MODEL), F32),
        compiler_params=_cparams("parallel"), name=prefix)(*args)


def _rope64(y, cos, sin):
    return y * cos + pltpu.roll(y, 64, 1) * sin


def _first_head_qk_lanes(rows):
    lane = lax.broadcasted_iota(jnp.int32, (rows, LANE), 1)
    return jnp.where(lane < 8, 1, jnp.where(lane < 16, 0, jnp.where(lane < 72, 1, 0))) > 0


def _store_regrouped(dil_ref, stage_ref, c, val):
    dilation, rows, _ = dil_ref.shape
    cols = slice(c * LANE, (c + 1) * LANE)
    if dilation == 1:
        dil_ref[0, :, cols] = val.astype(BF16)
        return
    stage_ref[c] = val
    for r in range(dilation):
        dil_ref[r, :, cols] = stage_ref[c, pl.ds(r, rows, stride=dilation), :].astype(BF16)


def _inproj_kernel(x_ref, gmix_ref, win_ref, gqa_ref, wuq_ref, gkva_ref, wuk_ref, wuv_ref,
                   gq_ref, gk_ref, gdq_ref, gdk_ref, cm_ref, sm_ref, cd_ref, sd_ref,
                   q_ref, k_ref, v_ref, dil0_ref, dil1_ref, dil2_ref, stage1_ref, stage2_ref):
    u = _rms(x_ref[...], gmix_ref[...]).astype(BF16)
    head = _dot(u, win_ref[:, :IN_HEAD])
    c_q = _rms(head[:, :Q_LORA], gqa_ref[...]).astype(BF16)
    c_kv = _rms(head[:, Q_LORA:Q_LORA + KV_LORA], gkva_ref[...]).astype(BF16)
    k_pe = head[:, Q_LORA + KV_LORA:]
    q_raw = _dot(c_q, wuq_ref[...])
    k_nope = _dot(c_kv, wuk_ref[...])
    v_ref[...] = _dot(c_kv, wuv_ref[...]).astype(BF16)

    cm, sm = cm_ref[...], sm_ref[...]
    gq, gk = gq_ref[...], gk_ref[...]
    k_rot = _rope64(k_pe * gk[:, LANE:], cm, sm)
    k_pe_ss = jnp.sum(k_pe * k_pe, axis=-1, keepdims=True)
    q_scale = QK_HEAD ** -0.5 * LOG2_E
    for h in range(MLA_HEADS):
        lo, mid, hi = h * HEAD_PAD, h * HEAD_PAD + LANE, (h + 1) * HEAD_PAD
        q0, q1 = q_raw[:, lo:mid], q_raw[:, mid:hi]
        ss = jnp.sum(q0 * q0 + q1 * q1, axis=-1, keepdims=True)
        r = lax.rsqrt(ss * (1.0 / QK_HEAD) + NORM_EPS) * q_scale
        q_ref[:, lo:mid] = (q0 * r * gq[:, :LANE]).astype(BF16)
        q_ref[:, mid:hi] = (_rope64(q1 * gq[:, LANE:], cm, sm) * r).astype(BF16)
        kn = k_nope[:, h * LANE:(h + 1) * LANE]
        ssk = jnp.sum(kn * kn, axis=-1, keepdims=True) + k_pe_ss
        rk = lax.rsqrt(ssk * (1.0 / QK_HEAD) + NORM_EPS)
        k_ref[:, lo:mid] = (kn * rk * gk[:, :LANE]).astype(BF16)
        k_ref[:, mid:hi] = (k_rot * rk).astype(BF16)

    cd, sd = cd_ref[...], sd_ref[...]
    in_a = _first_head_qk_lanes(u.shape[0])
    group_cols = 3 * DIL_WIDTH
    for g, (dil_ref, stage) in enumerate(zip((dil0_ref, dil1_ref, dil2_ref), (None, stage1_ref, stage2_ref))):
        base = g * group_cols
        store = functools.partial(_store_regrouped, dil_ref, stage)
        sec = _dot(u, win_ref[:, IN_HEAD + base:IN_HEAD + base + group_cols])
        for j, (gain_ref, scale) in enumerate(((gdq_ref, DIL_HEAD ** -0.5), (gdk_ref, 1.0))):
            gain = gain_ref[g:g + 1, :]
            for c in range(DIL_WIDTH // LANE):
                cols = slice(j * DIL_WIDTH + c * LANE, j * DIL_WIDTH + (c + 1) * LANE)
                xx = sec[:, cols]
                sq = xx * xx
                ss_a = jnp.sum(jnp.where(in_a, sq, 0.0), axis=-1, keepdims=True)
                ss_b = jnp.sum(jnp.where(in_a, 0.0, sq), axis=-1, keepdims=True)
                r_a = lax.rsqrt(ss_a * (1.0 / DIL_HEAD) + NORM_EPS)
                r_b = lax.rsqrt(ss_b * (1.0 / DIL_HEAD) + NORM_EPS)
                out = _rope64(xx * gain, cd, sd) * (jnp.where(in_a, r_a, r_b) * scale)
                store(j * DIL_WIDTH // LANE + c, out)
        for c in range(DIL_WIDTH // LANE):
            store(2 * DIL_WIDTH // LANE + c, sec[:, 2 * DIL_WIDTH + c * LANE:2 * DIL_WIDTH + (c + 1) * LANE])


def _inproj(x1, w, tabs, seq):
    t = x1.shape[0]
    tm = min(TM_PROJ, seq)
    nper = seq // tm
    row = lambda n: pl.BlockSpec((tm, n), lambda i: (i, 0))
    tab = pl.BlockSpec((tm, LANE), lambda i: (i % nper, 0))
    consts = [w['mix_norm'], w['w_in_main'], w['q_a_norm'], w['w_uq'], w['kv_a_norm'], w['w_uk'], w['w_uv'],
              w['mla_q_gain'], w['mla_k_gain'], w['dil_q_gain'], w['dil_k_gain']]
    widths = (MLA_HEADS * HEAD_PAD, MLA_HEADS * HEAD_PAD, MLA_HEADS * V_HEAD)
    group_cols = 3 * DIL_WIDTH
    dil_specs = [pl.BlockSpec((None, d, tm // d, group_cols), lambda i: (i // nper, 0, i % nper, 0))
                 for _, d in DIL_PATTERN]
    dil_shapes = [jax.ShapeDtypeStruct((t // seq, d, seq // d, group_cols), BF16) for _, d in DIL_PATTERN]
    return pl.pallas_call(
        _inproj_kernel, grid=(t // tm,),
        in_specs=[row(D_MODEL)] + [_const_spec(a.shape) for a in consts] + [tab] * 4,
        out_specs=[row(n) for n in widths] + dil_specs,
        out_shape=[jax.ShapeDtypeStruct((t, n), BF16) for n in widths] + dil_shapes,
        scratch_shapes=[pltpu.VMEM((group_cols // LANE, tm, LANE), F32)] * 2,
        compiler_params=_cparams("parallel"), name="inproj")(x1, *consts, *tabs)


def _mla_kernel(q_ref, k_ref, v_ref, o_ref, kt_ref, s_ref, *, tk, nk):
    tq = q_ref.shape[0]
    ones = jnp.ones((tk, LANE), BF16)

    def chunk(j):
        return pl.ds(j * tk if isinstance(j, int) else pl.multiple_of(j * tk, tk), tk)

    @pl.when(pl.program_id(2) == 0)
    def _():
        def transpose(j, _):
            kt_ref[:, chunk(j)] = k_ref[chunk(j), :].T
            return 0
        lax.fori_loop(0, nk, transpose, 0)

    def scores(j):
        return _dot(q_ref[...], kt_ref[:, chunk(j)])

    def update(j, s, m, acc):
        m_new = jnp.maximum(m, jnp.max(s, axis=-1, keepdims=True))
        p = jnp.exp2(s - m_new).astype(BF16)
        v_ext = jnp.concatenate([v_ref[chunk(j), :], ones], axis=1)
        return m_new, jnp.exp2(m - m_new) * acc + _dot(p, v_ext)

    m = jnp.full((tq, 1), -jnp.inf, F32)
    acc = jnp.zeros((tq, V_HEAD + LANE), F32)
    unroll = min(MLA_UNROLL, nk)
    assert nk % unroll == 0 and unroll % 2 == 0
    s_ref[0] = scores(0)

    def body(jj, carry):
        m, acc = carry
        for u in range(unroll):
            j = unroll * jj + u
            if u + 1 < unroll:
                s_ref[(u + 1) % 2] = scores(j + 1)
            elif nk > unroll:
                s_ref[0] = scores(jnp.minimum(j + 1, nk - 1))
            m, acc = update(j, s_ref[u % 2], m, acc)
        return m, acc

    if nk == unroll:
        m, acc = body(0, (m, acc))
    else:
        m, acc = lax.fori_loop(0, nk // unroll, body, (m, acc))
    o_ref[...] = (acc[:, :V_HEAD] / acc[:, V_HEAD:]).astype(BF16)


def _mla(q, k, v, batch, seq):
    t = q.shape[0]
    tq, tk = min(TQ_MLA, seq), min(TK_MLA, seq)
    nq = seq // tq
    return pl.pallas_call(
        functools.partial(_mla_kernel, tk=tk, nk=seq // tk),
        grid=(batch, MLA_HEADS, nq),
        in_specs=[pl.BlockSpec((tq, HEAD_PAD), lambda b, h, i: (b * nq + i, h)),
                  pl.BlockSpec((seq, HEAD_PAD), lambda b, h, i: (b, h)),
                  pl.BlockSpec((seq, V_HEAD), lambda b, h, i: (b, h))],
        out_specs=pl.BlockSpec((tq, V_HEAD), lambda b, h, i: (b * nq + i, h)),
        out_shape=jax.ShapeDtypeStruct((t, MLA_HEADS * V_HEAD), BF16),
        scratch_shapes=[pltpu.VMEM((HEAD_PAD, seq), BF16), pltpu.VMEM((2, tq, tk), F32)],
        compiler_params=_cparams("parallel", "parallel", "arbitrary"), name="mla")(q, k, v)


def _dil_kernel(q_ref, kp_ref, kc_ref, kn_ref, vp_ref, vc_ref, vn_ref, o_ref, lse_ref, *, tl, sub_len):
    i = pl.program_id(1)
    tk = tl + 2 * N_SIDE
    q = q_ref[...]
    k = jnp.concatenate([kp_ref[...], kc_ref[...], kn_ref[...]], axis=0)
    v = jnp.concatenate([vp_ref[...], vc_ref[...], vn_ref[...]], axis=0)
    q_idx = i * tl + lax.broadcasted_iota(jnp.int32, (tl, 1), 0)
    k_idx = i * tl - N_SIDE + lax.broadcasted_iota(jnp.int32, (1, tk), 1)
    lo = jnp.maximum(q_idx - N_SIDE, 0)
    hi = jnp.minimum(q_idx + N_SIDE, sub_len - 1)
    bias = jnp.where(k_idx >= lo, jnp.where(k_idx <= hi, 0.0, NEG_INF), NEG_INF)
    in_a = _first_head_qk_lanes(tl)
    first = lax.broadcasted_iota(jnp.int32, (tl, LANE), 1) < DIL_HEAD
    zero = jnp.zeros((), BF16)
    for c in range(DIL_WIDTH // LANE):
        cols = slice(c * LANE, (c + 1) * LANE)
        qp, kpair, vpair = q[:, cols], k[:, cols], v[:, cols]
        outs, lses = [], []
        for q_head in (jnp.where(in_a, qp, zero), jnp.where(in_a, zero, qp)):
            s = _dot_nt(q_head, kpair) + bias
            m = jnp.max(s, axis=-1, keepdims=True)
            p = jnp.exp(s - m)
            l = jnp.sum(p, axis=-1, keepdims=True)
            outs.append(_dot(p.astype(BF16), vpair) / l)
            lses.append(m + jnp.log(l))
        o_ref[:, cols] = jnp.where(first, outs[0], outs[1])
        lse_ref[:, cols] = jnp.where(first, lses[0], lses[1])


def _dilated_group(dil, g):
    batch, dilation, sub_len, group_cols = dil.shape
    n_sub = batch * dilation
    tl = min(TL_DIL, sub_len)
    nblk = sub_len // tl
    halo_per_blk = tl // N_SIDE
    n_halo = sub_len // N_SIDE
    rows2d = dil.reshape(n_sub * sub_len, group_cols)

    def spec(rows, part, blk, per_sub):
        return pl.BlockSpec((rows, DIL_WIDTH), lambda s, i: (s * per_sub + blk(i), part))

    cur = lambda i: i
    prev = lambda i: jnp.maximum(i * halo_per_blk - 1, 0)
    nxt = lambda i: jnp.minimum((i + 1) * halo_per_blk, n_halo - 1)
    out_spec = spec(tl, 0, cur, nblk)
    out_sds = jax.ShapeDtypeStruct((n_sub * sub_len, DIL_WIDTH), F32)
    o, lse = pl.pallas_call(
        functools.partial(_dil_kernel, tl=tl, sub_len=sub_len),
        grid=(n_sub, nblk),
        in_specs=[spec(tl, 0, cur, nblk),
                  spec(N_SIDE, 1, prev, n_halo), spec(tl, 1, cur, nblk), spec(N_SIDE, 1, nxt, n_halo),
                  spec(N_SIDE, 2, prev, n_halo), spec(tl, 2, cur, nblk), spec(N_SIDE, 2, nxt, n_halo)],
        out_specs=[out_spec, out_spec], out_shape=[out_sds, out_sds],
        compiler_params=_cparams("parallel", "arbitrary"), name=f"dilated{g}")(*([rows2d] * 7))
    shape4d = (batch, dilation, sub_len, DIL_WIDTH)
    return o.reshape(shape4d), lse.reshape(shape4d)


def _token_order(blk_ref, stage_ref):
    dilation, rows, n = blk_ref.shape
    if dilation == 1:
        return blk_ref[0]
    for c in range(n // LANE):
        for r in range(dilation):
            stage_ref[c, pl.ds(r, rows, stride=dilation), :] = blk_ref[r, :, c * LANE:(c + 1) * LANE]
    return jnp.concatenate([stage_ref[c] for c in range(n // LANE)], axis=1)


def _merge_kernel(x_ref, om_ref, o0_ref, o1_ref, o2_ref, l0_ref, l1_ref, l2_ref,
                  gmix_ref, wgate_ref, woa_ref, wob_ref, wout_ref, y_ref, *stage_refs):
    x = x_ref[...]
    outs = [_token_order(o0_ref, None), _token_order(o1_ref, stage_refs[0]), _token_order(o2_ref, stage_refs[1])]
    lses = [_token_order(l0_ref, None), _token_order(l1_ref, stage_refs[2]), _token_order(l2_ref, stage_refs[3])]
    top = jnp.maximum(jnp.maximum(lses[0], lses[1]), lses[2])
    e = [jnp.exp(l - top) for l in lses]
    o_dil = (e[0] * outs[0] + e[1] * outs[1] + e[2] * outs[2]) / (e[0] + e[1] + e[2])
    u = _rms(x, gmix_ref[...]).astype(BF16)
    gates = jax.nn.sigmoid(_dot(u, wgate_ref[...]))
    a = _dot(om_ref[...], woa_ref[...])
    b = _dot(o_dil.astype(BF16), wob_ref[...])
    mix = gates[:, :D_MODEL] * a + gates[:, D_MODEL:] * b
    y_ref[...] = x + _dot(mix.astype(BF16), wout_ref[...])


def _merge(x1, o_mla, dil_outs, w, seq):
    t = x1.shape[0]
    tm = min(TM_MERGE, seq)
    nper = seq // tm
    row = lambda n: pl.BlockSpec((tm, n), lambda i: (i, 0))
    outs = [o for o, _ in dil_outs]
    lses = [l for _, l in dil_outs]
    sub_specs = [pl.BlockSpec((None, d, tm // d, DIL_WIDTH), lambda i: (i // nper, 0, i % nper, 0))
                 for _, d in DIL_PATTERN]
    consts = [w['mix_norm'], w['w_gates'], w['w_o_mla'], w['w_o_dil'], w['w_out']]
    return pl.pallas_call(
        _merge_kernel, grid=(t // tm,),
        in_specs=[row(D_MODEL), row(MLA_HEADS * V_HEAD)] + sub_specs * 2
                 + [_const_spec(a.shape) for a in consts],
        out_specs=row(D_MODEL), out_shape=jax.ShapeDtypeStruct((t, D_MODEL), F32),
        scratch_shapes=[pltpu.VMEM((DIL_WIDTH // LANE, tm, LANE), F32)] * 4,
        compiler_params=_cparams("parallel"), name="merge")(x1, o_mla, *outs, *lses, *consts)


def _pair_perm():
    r = DIL_ROT // 2
    a, b = np.arange(DIL_HEAD), DIL_HEAD + np.arange(DIL_HEAD)
    return np.concatenate([a[:r], b[:r], a[2 * r:], a[r:2 * r], b[r:2 * r], b[2 * r:]])


def _rope_pad_cols():
    half = QK_ROPE // 2
    cols = -np.ones(LANE, np.int64)
    cols[:half] = np.arange(half)
    cols[LANE // 2:LANE // 2 + half] = half + np.arange(half)
    return cols


def _take_cols(a, cols):
    cols = np.asarray(cols)
    out = jnp.take(a, jnp.asarray(np.maximum(cols, 0)), axis=-1)
    return jnp.where(jnp.asarray(cols >= 0), out, 0)


def _prep_weights(w):
    pad = _rope_pad_cols()
    perm = _pair_perm()
    out = dict(w)
    for name in ('ffn1_norm', 'mix_norm', 'q_a_norm', 'kv_a_norm', 'ffn2_norm', 'ple_norm'):
        out[name] = w[name].reshape(1, -1)
    for name in ('ffn1_w_gate', 'ffn1_w_up', 'ffn1_w_down', 'ffn2_w_gate', 'ffn2_w_up', 'ffn2_w_down',
                 'w_o_mla', 'w_o_dil', 'w_out', 'w_ple', 'w_ple_gate'):
        out[name] = w[name].astype(BF16)

    w_in = w['w_in']
    o_kpe, o_dil, o_gate = Q_LORA + KV_LORA, Q_LORA + KV_LORA + QK_ROPE, Q_LORA + KV_LORA + QK_ROPE + DIL_QKV
    dil_cols = []
    for g in range(N_GROUPS):
        for part in range(3):
            for c in range(DIL_WIDTH // LANE):
                base = o_dil + (3 * g + part) * DIL_WIDTH + c * LANE
                dil_cols.append(base + (perm if part < 2 else np.arange(LANE)))
    main_cols = np.concatenate([np.arange(o_kpe), np.where(pad >= 0, o_kpe + pad, -1)] + dil_cols)
    out['w_in_main'] = _take_cols(w_in, main_cols).astype(BF16)
    out['w_gates'] = w_in[:, o_gate:].astype(BF16)

    head_cols = np.concatenate([np.arange(QK_NOPE), np.where(pad >= 0, QK_NOPE + pad, -1)])
    uq_cols = np.concatenate([np.where(head_cols >= 0, h * QK_HEAD + head_cols, -1) for h in range(MLA_HEADS)])
    out['w_uq'] = _take_cols(w['w_uq'], uq_cols).astype(BF16)
    ukv = w['w_ukv'].reshape(KV_LORA, MLA_HEADS, QK_NOPE + V_HEAD)
    out['w_uk'] = ukv[:, :, :QK_NOPE].reshape(KV_LORA, -1).astype(BF16)
    out['w_uv'] = ukv[:, :, QK_NOPE:].reshape(KV_LORA, -1).astype(BF16)
    out['mla_q_gain'] = _take_cols(w['mla_q_norm'], head_cols).reshape(1, -1)
    out['mla_k_gain'] = _take_cols(w['mla_k_norm'], head_cols).reshape(1, -1)
    out['dil_q_gain'] = _take_cols(w['dil_q_norm'], perm % DIL_HEAD)
    out['dil_k_gain'] = _take_cols(w['dil_k_norm'], perm % DIL_HEAD)
    return out


def _rope_tables(seq):
    pos = jnp.arange(seq, dtype=F32)[:, None]

    def angles(r):
        inv = ROPE_THETA ** (-jnp.arange(0, r, 2, dtype=F32) / r)
        ang = pos * inv[None, :]
        return jnp.cos(ang), jnp.sin(ang)

    cos, sin = angles(QK_ROPE)
    half = QK_ROPE // 2
    zeros = jnp.zeros((seq, LANE // 2 - half), F32)
    cm = jnp.concatenate([cos, zeros, cos, zeros], axis=1)
    sm = jnp.concatenate([-sin, zeros, sin, zeros], axis=1)

    cos, sin = angles(DIL_ROT)
    rest = LANE // 2 - DIL_ROT
    cd = jnp.concatenate([cos, cos, jnp.ones((seq, rest), F32)] * 2, axis=1)
    sd = jnp.concatenate([-sin, -sin, jnp.zeros((seq, rest), F32), sin, sin, jnp.zeros((seq, rest), F32)], axis=1)
    return cm, sm, cd, sd


def _encoder_layer(x, p, w):
    batch, seq, _ = x.shape
    x = x.reshape(batch * seq, D_MODEL)
    p = p.reshape(batch * seq, PLE_DIM)
    x1 = _ffn(x, w, 'ffn1')
    q, k, v, *dil = _inproj(x1, w, _rope_tables(seq), seq)
    o_mla = _mla(q, k, v, batch, seq)
    dil_outs = [_dilated_group(dil[g], g) for g in range(N_GROUPS)]
    x2 = _merge(x1, o_mla, dil_outs, w, seq)
    return _ffn(x2, w, 'ffn2', p).reshape(batch, seq, D_MODEL)


def kernel(x_prompt, x_sample, p_prompt, p_sample, ffn1_norm, ffn1_w_gate, ffn1_w_up, ffn1_w_down, mix_norm, w_in, q_a_norm, w_uq, kv_a_norm, w_ukv, mla_q_norm, mla_k_norm, dil_q_norm, dil_k_norm, w_o_mla, w_o_dil, w_out, ffn2_norm, ffn2_w_gate, ffn2_w_up, ffn2_w_down, ple_norm, w_ple, w_ple_gate):
    stacked = dict(ffn1_norm=ffn1_norm, ffn1_w_gate=ffn1_w_gate, ffn1_w_up=ffn1_w_up, ffn1_w_down=ffn1_w_down,
                   mix_norm=mix_norm, w_in=w_in, q_a_norm=q_a_norm, w_uq=w_uq, kv_a_norm=kv_a_norm, w_ukv=w_ukv,
                   mla_q_norm=mla_q_norm, mla_k_norm=mla_k_norm, dil_q_norm=dil_q_norm, dil_k_norm=dil_k_norm,
                   w_o_mla=w_o_mla, w_o_dil=w_o_dil, w_out=w_out, ffn2_norm=ffn2_norm, ffn2_w_gate=ffn2_w_gate,
                   ffn2_w_up=ffn2_w_up, ffn2_w_down=ffn2_w_down, ple_norm=ple_norm, w_ple=w_ple,
                   w_ple_gate=w_ple_gate)
    depth = ffn1_norm.shape[0]
    layers = [_prep_weights({name: a[i] for name, a in stacked.items()}) for i in range(depth)]

    def run(x, p):
        for i in range(depth):
            x = _encoder_layer(x, p[i], layers[i])
        return x

    return run(x_prompt, p_prompt), run(x_sample, p_sample)
```

```python
import functools

import numpy as np
import jax
import jax.numpy as jnp
from jax import lax
from jax.experimental import pallas as pl
from jax.experimental.pallas import tpu as pltpu

F32 = jnp.float32
BF16 = jnp.bfloat16

D_MODEL = 1024
PLE_DIM = 256
D_FF = 2816
ROPE_THETA = 500000.0
NORM_EPS = 1e-6
NEG_INF = -1e30
MLA_HEADS = 8
Q_LORA = 512
KV_LORA = 256
QK_NOPE = 128
QK_ROPE = 64
QK_HEAD = QK_NOPE + QK_ROPE
V_HEAD = 128
DIL_PATTERN = ((128, 1), (512, 4), (2048, 16))
N_GROUPS = 3
DIL_HEADS = 8
DIL_HEAD = 64
DIL_ROT = DIL_HEAD // 4
DIL_WIDTH = DIL_HEADS * DIL_HEAD
DIL_QKV = 3 * N_GROUPS * DIL_WIDTH
N_SIDE = 64
assert all(w // (2 * d) == N_SIDE for w, d in DIL_PATTERN)

LOG2_E = 1.4426950408889634
LANE = 128
HEAD_PAD = 2 * LANE
IN_HEAD = Q_LORA + KV_LORA + LANE
VMEM_LIMIT = 56 * 1024 * 1024

TM_FFN = 512
TM_PROJ = 256
TM_MERGE = 256
TQ_MLA = 512
TK_MLA = 1024
TL_DIL = 512
DIL_SUB = 128
FF_CHUNK = D_FF // 2
MLA_UNROLL = 4
EPI_ROWS = 64


def _cparams(*sem):
    return pltpu.CompilerParams(dimension_semantics=sem, vmem_limit_bytes=VMEM_LIMIT)


def _const_spec(shape):
    nd = len(shape)
    return pl.BlockSpec(shape, lambda *_: (0,) * nd, pipeline_mode=pl.Buffered(1))


def _rms(x, g):
    return x * lax.rsqrt(jnp.mean(x * x, axis=-1, keepdims=True) + NORM_EPS) * g


def _dot(a, b):
    return jnp.dot(a, b, preferred_element_type=F32)


def _dot_nt(a, b):
    return lax.dot_general(a, b, (((1,), (1,)), ((), ())), preferred_element_type=F32)


def _ffn_body(x, g_ref, wg_ref, wu_ref, wd_ref):
    h = _rms(x, g_ref[...]).astype(BF16)
    acc = jnp.zeros_like(x)
    for c in range(D_FF // FF_CHUNK):
        cols = slice(c * FF_CHUNK, (c + 1) * FF_CHUNK)
        gate = _dot(h, wg_ref[:, cols])
        up = _dot(h, wu_ref[:, cols])
        act = (gate * jax.nn.sigmoid(gate) * up).astype(BF16)
        acc = acc + _dot(act, wd_ref[cols, :])
    return x + 0.5 * acc


def _ffn_kernel(x_ref, g_ref, wg_ref, wu_ref, wd_ref, o_ref):
    o_ref[...] = _ffn_body(x_ref[...], g_ref, wg_ref, wu_ref, wd_ref)


def _ffn_ple_kernel(x_ref, p_ref, g_ref, wg_ref, wu_ref, wd_ref, gp_ref, wpg_ref, wpe_ref, o_ref):
    y = _ffn_body(x_ref[...], g_ref, wg_ref, wu_ref, wd_ref)
    gate = jax.nn.sigmoid(_dot(_rms(y, gp_ref[...]).astype(BF16), wpg_ref[...]))
    o_ref[...] = y + gate * _dot(p_ref[...].astype(BF16), wpe_ref[...])


def _ffn(x, w, prefix, p=None):
    t = x.shape[0]
    tm = min(TM_FFN, t)
    row = pl.BlockSpec((tm, D_MODEL), lambda i: (i, 0))
    ffn_w = [w[prefix + '_norm'], w[prefix + '_w_gate'], w[prefix + '_w_up'], w[prefix + '_w_down']]
    if p is None:
        kern, args, specs = _ffn_kernel, [x] + ffn_w, [row]
    else:
        kern, args = _ffn_ple_kernel, [x, p] + ffn_w + [w['ple_norm'], w['w_ple_gate'], w['w_ple']]
        specs = [row, pl.BlockSpec((tm, PLE_DIM), lambda i: (i, 0))]
    specs = specs + [_const_spec(a.shape) for a in args[len(specs):]]
    return pl.pallas_call(
        kern, grid=(t // tm,), in_specs=specs, out_specs=row,
        out_shape=jax.ShapeDtypeStruct((t, D_MODEL), F32),
        compiler_params=_cparams("parallel"), name=prefix)(*args)


def _rope64(y, cos, sin):
    return y * cos + pltpu.roll(y, 64, 1) * sin


def _first_head_qk_lanes(rows):
    lane = lax.broadcasted_iota(jnp.int32, (rows, LANE), 1)
    return jnp.where(lane < 8, 1, jnp.where(lane < 16, 0, jnp.where(lane < 72, 1, 0))) > 0


def _store_regrouped(dil_ref, stage_ref, c):
    dilation, rows, _ = dil_ref.shape
    cols = slice(c * LANE, (c + 1) * LANE)
    if dilation == 1:
        dil_ref[0, :, cols] = stage_ref[...].astype(BF16)
        return
    for r in range(dilation):
        dil_ref[r, :, cols] = stage_ref[pl.ds(r, rows, stride=dilation), :].astype(BF16)


RAW_Q = 0
RAW_K = RAW_Q + 2 * MLA_HEADS
RAW_KPE = RAW_K + MLA_HEADS
RAW_DIL = RAW_KPE + 1
RAW_CHUNKS = RAW_DIL + DIL_QKV // LANE


def _inproj_kernel(x_ref, gmix_ref, win_ref, gqa_ref, wuq_ref, gkva_ref, wuk_ref, wuv_ref,
                   gq_ref, gk_ref, gdq_ref, gdk_ref, cm_ref, sm_ref, cd_ref, sd_ref,
                   seg_ref, q_ref, k_ref, v_ref, dil0_ref, dil1_ref, dil2_ref, raw_ref, rawv_ref):
    i = pl.program_id(0)
    cur = lax.rem(i, 2)
    prev = 1 - cur
    tm = x_ref.shape[0]

    @pl.when(i == 0)
    def _():
        raw_ref[1] = jnp.zeros(raw_ref.shape[1:], F32)
        rawv_ref[1] = jnp.zeros(rawv_ref.shape[1:], BF16)

    def put(c0, val):
        for c in range(val.shape[1] // LANE):
            raw_ref[cur, c0 + c] = val[:, c * LANE:(c + 1) * LANE]

    u = _rms(x_ref[...], gmix_ref[...]).astype(BF16)
    head = _dot(u, win_ref[:, :IN_HEAD])
    c_q = _rms(head[:, :Q_LORA], gqa_ref[...]).astype(BF16)
    c_kv = _rms(head[:, Q_LORA:Q_LORA + KV_LORA], gkva_ref[...]).astype(BF16)
    put(RAW_KPE, head[:, Q_LORA + KV_LORA:])
    put(RAW_Q, _dot(c_q, wuq_ref[...]))
    put(RAW_K, _dot(c_kv, wuk_ref[...]))
    rawv_ref[cur] = _dot(c_kv, wuv_ref[...]).astype(BF16)
    group_cols = 3 * DIL_WIDTH
    chunks = DIL_WIDTH // LANE
    seg = seg_ref[...]
    for g in range(N_GROUPS):
        base = IN_HEAD + g * group_cols
        sec = _dot(u, win_ref[:, base:base + group_cols])
        for j, scale in enumerate((DIL_HEAD ** -0.5 * LOG2_E, 1.0)):
            for c in range(0, chunks, 2):
                x2 = sec[:, (j * chunks + c) * LANE:(j * chunks + c + 2) * LANE]
                ss = _dot((x2 * x2).astype(BF16), seg)
                put(RAW_DIL + g * 3 * chunks + j * chunks + c,
                    x2 * (lax.rsqrt(ss * (1.0 / DIL_HEAD) + NORM_EPS) * scale))
        put(RAW_DIL + g * 3 * chunks + 2 * chunks, sec[:, 2 * DIL_WIDTH:])

    v_ref[...] = rawv_ref[prev]
    gq, gk = gq_ref[...], gk_ref[...]
    q_scale = QK_HEAD ** -0.5 * LOG2_E
    dil_refs = (dil0_ref, dil1_ref, dil2_ref)
    for r0 in range(0, tm, EPI_ROWS):
        rows = pl.ds(r0, EPI_ROWS)
        get = lambda c: raw_ref[prev, c, rows, :]
        cm, sm = cm_ref[rows, :], sm_ref[rows, :]
        k_pe = get(RAW_KPE)
        k_rot = _rope64(k_pe * gk[:, LANE:], cm, sm)
        k_pe_ss = jnp.sum(k_pe * k_pe, axis=-1, keepdims=True)
        for h in range(MLA_HEADS):
            lo, mid, hi = h * HEAD_PAD, h * HEAD_PAD + LANE, (h + 1) * HEAD_PAD
            q0, q1 = get(RAW_Q + 2 * h), get(RAW_Q + 2 * h + 1)
            ss = jnp.sum(q0 * q0 + q1 * q1, axis=-1, keepdims=True)
            r = lax.rsqrt(ss * (1.0 / QK_HEAD) + NORM_EPS) * q_scale
            q_ref[rows, lo:mid] = (q0 * r * gq[:, :LANE]).astype(BF16)
            q_ref[rows, mid:hi] = (_rope64(q1 * gq[:, LANE:], cm, sm) * r).astype(BF16)
            kn = get(RAW_K + h)
            ssk = jnp.sum(kn * kn, axis=-1, keepdims=True) + k_pe_ss
            rk = lax.rsqrt(ssk * (1.0 / QK_HEAD) + NORM_EPS)
            k_ref[rows, lo:mid] = (kn * rk * gk[:, :LANE]).astype(BF16)
            k_ref[rows, mid:hi] = (k_rot * rk).astype(BF16)

        cd, sd = cd_ref[rows, :], sd_ref[rows, :]
        for g, dil_ref in enumerate(dil_refs):
            for j, gain_ref in enumerate((gdq_ref, gdk_ref)):
                gain = gain_ref[g:g + 1, :]
                for c in range(j * chunks, (j + 1) * chunks):
                    cid = RAW_DIL + g * 3 * chunks + c
                    out = _rope64(get(cid) * gain, cd, sd)
                    if dil_ref.shape[0] == 1:
                        dil_ref[0, rows, c * LANE:(c + 1) * LANE] = out.astype(BF16)
                    else:
                        raw_ref[prev, cid, rows, :] = out

    for g, dil_ref in enumerate(dil_refs):
        first = RAW_DIL + g * 3 * chunks
        done = 2 * chunks if dil_ref.shape[0] == 1 else 0
        for c in range(done, 3 * chunks):
            _store_regrouped(dil_ref, raw_ref.at[prev, first + c], c)


def _inproj(x1, w, tabs, seq):
    t = x1.shape[0]
    tm = min(TM_PROJ, seq)
    nper = seq // tm
    ntile = t // tm
    lag = lambda i: jnp.maximum(i - 1, 0)
    row = lambda n: pl.BlockSpec((tm, n), lambda i: (lag(i), 0))
    tab = pl.BlockSpec((tm, LANE), lambda i: (lag(i) % nper, 0))
    consts = [w['mix_norm'], w['w_in_main'], w['q_a_norm'], w['w_uq'], w['kv_a_norm'], w['w_uk'], w['w_uv'],
              w['mla_q_gain'], w['mla_k_gain'], w['dil_q_gain'], w['dil_k_gain']]
    widths = (MLA_HEADS * HEAD_PAD, MLA_HEADS * HEAD_PAD, MLA_HEADS * V_HEAD)
    group_cols = 3 * DIL_WIDTH
    dil_specs = [pl.BlockSpec((None, d, tm // d, group_cols), lambda i: (lag(i) // nper, 0, lag(i) % nper, 0))
                 for _, d in DIL_PATTERN]
    dil_shapes = [jax.ShapeDtypeStruct((t // seq, d, seq // d, group_cols), BF16) for _, d in DIL_PATTERN]
    return pl.pallas_call(
        _inproj_kernel, grid=(ntile + 1,),
        in_specs=[pl.BlockSpec((tm, D_MODEL), lambda i: (jnp.minimum(i, ntile - 1), 0))]
                 + [_const_spec(a.shape) for a in consts] + [tab] * 4 + [_const_spec(w['dil_seg'].shape)],
        out_specs=[row(n) for n in widths] + dil_specs,
        out_shape=[jax.ShapeDtypeStruct((t, n), BF16) for n in widths] + dil_shapes,
        scratch_shapes=[pltpu.VMEM((2, RAW_CHUNKS, tm, LANE), F32), pltpu.VMEM((2, tm, MLA_HEADS * V_HEAD), BF16)],
        compiler_params=_cparams("arbitrary"), name="inproj")(x1, *consts, *tabs, w['dil_seg'])


def _mla_kernel(q_ref, k_ref, v_ref, o_ref, kt_ref, s_ref, *, tk, nk):
    tq = q_ref.shape[0]
    ones = jnp.ones((tk, LANE), BF16)

    def chunk(j):
        return pl.ds(j * tk if isinstance(j, int) else pl.multiple_of(j * tk, tk), tk)

    @pl.when(pl.program_id(2) == 0)
    def _():
        def transpose(j, _):
            kt_ref[:, chunk(j)] = k_ref[chunk(j), :].T
            return 0
        lax.fori_loop(0, nk, transpose, 0)

    def scores(j):
        return _dot(q_ref[...], kt_ref[:, chunk(j)])

    def update(j, s, m, acc):
        m_new = jnp.maximum(m, jnp.max(s, axis=-1, keepdims=True))
        p = jnp.exp2(s - m_new).astype(BF16)
        v_ext = jnp.concatenate([v_ref[chunk(j), :], ones], axis=1)
        return m_new, jnp.exp2(m - m_new) * acc + _dot(p, v_ext)

    m = jnp.full((tq, 1), -jnp.inf, F32)
    acc = jnp.zeros((tq, V_HEAD + LANE), F32)
    unroll = min(MLA_UNROLL, nk)
    assert nk % unroll == 0 and unroll % 2 == 0
    s_ref[0] = scores(0)

    def body(jj, carry):
        m, acc = carry
        for u in range(unroll):
            j = unroll * jj + u
            if u + 1 < unroll:
                s_ref[(u + 1) % 2] = scores(j + 1)
            elif nk > unroll:
                s_ref[0] = scores(jnp.minimum(j + 1, nk - 1))
            m, acc = update(j, s_ref[u % 2], m, acc)
        return m, acc

    if nk == unroll:
        m, acc = body(0, (m, acc))
    else:
        m, acc = lax.fori_loop(0, nk // unroll, body, (m, acc))
    o_ref[...] = (acc[:, :V_HEAD] / acc[:, V_HEAD:]).astype(BF16)


def _mla(q, k, v, batch, seq):
    t = q.shape[0]
    tq, tk = min(TQ_MLA, seq), min(TK_MLA, seq)
    nq = seq // tq
    return pl.pallas_call(
        functools.partial(_mla_kernel, tk=tk, nk=seq // tk),
        grid=(batch, MLA_HEADS, nq),
        in_specs=[pl.BlockSpec((tq, HEAD_PAD), lambda b, h, i: (b * nq + i, h)),
                  pl.BlockSpec((seq, HEAD_PAD), lambda b, h, i: (b, h)),
                  pl.BlockSpec((seq, V_HEAD), lambda b, h, i: (b, h))],
        out_specs=pl.BlockSpec((tq, V_HEAD), lambda b, h, i: (b * nq + i, h)),
        out_shape=jax.ShapeDtypeStruct((t, MLA_HEADS * V_HEAD), BF16),
        scratch_shapes=[pltpu.VMEM((HEAD_PAD, seq), BF16), pltpu.VMEM((2, tq, tk), F32)],
        compiler_params=_cparams("parallel", "parallel", "arbitrary"), name="mla")(q, k, v)


def _dil_kernel(q_ref, kp_ref, kc_ref, kn_ref, vp_ref, vc_ref, vn_ref, o_ref, lse_ref, *, tl, sub_len):
    i = pl.program_id(1)
    sb = min(DIL_SUB, tl)
    wk = sb + 2 * N_SIDE
    k = jnp.concatenate([kp_ref[...], kc_ref[...], kn_ref[...]], axis=0)
    v = jnp.concatenate([vp_ref[...], vc_ref[...], vn_ref[...]], axis=0)
    in_a = _first_head_qk_lanes(sb)
    first = lax.broadcasted_iota(jnp.int32, (sb, LANE), 1) < DIL_HEAD
    zero = jnp.zeros((), BF16)
    ones = jnp.ones((wk, LANE), BF16)
    out_rows, lse_rows = [], []
    for r0 in range(0, tl, sb):
        q_idx = i * tl + r0 + lax.broadcasted_iota(jnp.int32, (sb, 1), 0)
        k_idx = i * tl + r0 - N_SIDE + lax.broadcasted_iota(jnp.int32, (1, wk), 1)
        lo = jnp.maximum(q_idx - N_SIDE, 0)
        hi = jnp.minimum(q_idx + N_SIDE, sub_len - 1)
        bias = jnp.where(k_idx >= lo, jnp.where(k_idx <= hi, 0.0, NEG_INF), NEG_INF)
        bias2 = jnp.concatenate([bias, bias], axis=0)
        outs, lses = [], []
        for c in range(DIL_WIDTH // LANE):
            cols = slice(c * LANE, (c + 1) * LANE)
            qp, kpair = q_ref[r0:r0 + sb, cols], k[r0:r0 + wk, cols]
            v_ext = jnp.concatenate([v[r0:r0 + wk, cols], ones], axis=1)
            q2 = jnp.concatenate([jnp.where(in_a, qp, zero), jnp.where(in_a, zero, qp)], axis=0)
            s = _dot_nt(q2, kpair) + bias2
            m = jnp.max(s, axis=-1, keepdims=True)
            pv = _dot(jnp.exp2(s - m).astype(BF16), v_ext)
            l = pv[:, LANE:]
            o2 = pv[:, :LANE] / l
            lse2 = m + jnp.log2(l)
            outs.append(jnp.where(first, o2[:sb], o2[sb:]))
            lses.append(jnp.where(first, lse2[:sb], lse2[sb:]))
        out_rows.append(jnp.concatenate(outs, axis=1))
        lse_rows.append(jnp.concatenate(lses, axis=1))
    o_ref[...] = jnp.concatenate(out_rows, axis=0)
    lse_ref[...] = jnp.concatenate(lse_rows, axis=0)


def _dilated_group(dil, g):
    batch, dilation, sub_len, group_cols = dil.shape
    n_sub = batch * dilation
    tl = min(TL_DIL, sub_len)
    nblk = sub_len // tl
    halo_per_blk = tl // N_SIDE
    n_halo = sub_len // N_SIDE
    rows2d = dil.reshape(n_sub * sub_len, group_cols)

    def spec(rows, part, blk, per_sub):
        return pl.BlockSpec((rows, DIL_WIDTH), lambda s, i: (s * per_sub + blk(i), part))

    cur = lambda i: i
    prev = lambda i: jnp.maximum(i * halo_per_blk - 1, 0)
    nxt = lambda i: jnp.minimum((i + 1) * halo_per_blk, n_halo - 1)
    out_spec = spec(tl, 0, cur, nblk)
    out_sds = jax.ShapeDtypeStruct((n_sub * sub_len, DIL_WIDTH), F32)
    o, lse = pl.pallas_call(
        functools.partial(_dil_kernel, tl=tl, sub_len=sub_len),
        grid=(n_sub, nblk),
        in_specs=[spec(tl, 0, cur, nblk),
                  spec(N_SIDE, 1, prev, n_halo), spec(tl, 1, cur, nblk), spec(N_SIDE, 1, nxt, n_halo),
                  spec(N_SIDE, 2, prev, n_halo), spec(tl, 2, cur, nblk), spec(N_SIDE, 2, nxt, n_halo)],
        out_specs=[out_spec, out_spec], out_shape=[out_sds, out_sds],
        compiler_params=_cparams("parallel", "arbitrary"), name=f"dilated{g}")(*([rows2d] * 7))
    shape4d = (batch, dilation, sub_len, DIL_WIDTH)
    return o.reshape(shape4d), lse.reshape(shape4d)


def _token_order(blk_ref, stage_ref):
    dilation, rows, n = blk_ref.shape
    if dilation == 1:
        return blk_ref[0]
    for c in range(n // LANE):
        for r in range(dilation):
            stage_ref[c, pl.ds(r, rows, stride=dilation), :] = blk_ref[r, :, c * LANE:(c + 1) * LANE]
    return jnp.concatenate([stage_ref[c] for c in range(n // LANE)], axis=1)


def _merge_kernel(x_ref, om_ref, o0_ref, o1_ref, o2_ref, l0_ref, l1_ref, l2_ref,
                  gmix_ref, wgate_ref, woa_ref, wob_ref, wout_ref, y_ref, *stage_refs):
    x = x_ref[...]
    outs = [_token_order(o0_ref, None), _token_order(o1_ref, stage_refs[0]), _token_order(o2_ref, stage_refs[1])]
    lses = [_token_order(l0_ref, None), _token_order(l1_ref, stage_refs[2]), _token_order(l2_ref, stage_refs[3])]
    top = jnp.maximum(jnp.maximum(lses[0], lses[1]), lses[2])
    e = [jnp.exp2(l - top) for l in lses]
    o_dil = (e[0] * outs[0] + e[1] * outs[1] + e[2] * outs[2]) / (e[0] + e[1] + e[2])
    u = _rms(x, gmix_ref[...]).astype(BF16)
    gates = jax.nn.sigmoid(_dot(u, wgate_ref[...]))
    a = _dot(om_ref[...], woa_ref[...])
    b = _dot(o_dil.astype(BF16), wob_ref[...])
    mix = gates[:, :D_MODEL] * a + gates[:, D_MODEL:] * b
    y_ref[...] = x + _dot(mix.astype(BF16), wout_ref[...])


def _merge(x1, o_mla, dil_outs, w, seq):
    t = x1.shape[0]
    tm = min(TM_MERGE, seq)
    nper = seq // tm
    row = lambda n: pl.BlockSpec((tm, n), lambda i: (i, 0))
    outs = [o for o, _ in dil_outs]
    lses = [l for _, l in dil_outs]
    sub_specs = [pl.BlockSpec((None, d, tm // d, DIL_WIDTH), lambda i: (i // nper, 0, i % nper, 0))
                 for _, d in DIL_PATTERN]
    consts = [w['mix_norm'], w['w_gates'], w['w_o_mla'], w['w_o_dil'], w['w_out']]
    return pl.pallas_call(
        _merge_kernel, grid=(t // tm,),
        in_specs=[row(D_MODEL), row(MLA_HEADS * V_HEAD)] + sub_specs * 2
                 + [_const_spec(a.shape) for a in consts],
        out_specs=row(D_MODEL), out_shape=jax.ShapeDtypeStruct((t, D_MODEL), F32),
        scratch_shapes=[pltpu.VMEM((DIL_WIDTH // LANE, tm, LANE), F32)] * 4,
        compiler_params=_cparams("parallel"), name="merge")(x1, o_mla, *outs, *lses, *consts)


def _pair_perm():
    r = DIL_ROT // 2
    a, b = np.arange(DIL_HEAD), DIL_HEAD + np.arange(DIL_HEAD)
    return np.concatenate([a[:r], b[:r], a[2 * r:], a[r:2 * r], b[r:2 * r], b[2 * r:]])


def _rope_pad_cols():
    half = QK_ROPE // 2
    cols = -np.ones(LANE, np.int64)
    cols[:half] = np.arange(half)
    cols[LANE // 2:LANE // 2 + half] = half + np.arange(half)
    return cols


def _take_cols(a, cols):
    cols = np.asarray(cols)
    out = jnp.take(a, jnp.asarray(np.maximum(cols, 0)), axis=-1)
    return jnp.where(jnp.asarray(cols >= 0), out, 0)


def _prep_weights(w):
    pad = _rope_pad_cols()
    perm = _pair_perm()
    out = dict(w)
    for name in ('ffn1_norm', 'mix_norm', 'q_a_norm', 'kv_a_norm', 'ffn2_norm', 'ple_norm'):
        out[name] = w[name].reshape(1, -1)
    for name in ('ffn1_w_gate', 'ffn1_w_up', 'ffn1_w_down', 'ffn2_w_gate', 'ffn2_w_up', 'ffn2_w_down',
                 'w_o_mla', 'w_o_dil', 'w_out', 'w_ple', 'w_ple_gate'):
        out[name] = w[name].astype(BF16)

    w_in = w['w_in']
    o_kpe, o_dil, o_gate = Q_LORA + KV_LORA, Q_LORA + KV_LORA + QK_ROPE, Q_LORA + KV_LORA + QK_ROPE + DIL_QKV
    dil_cols = []
    for g in range(N_GROUPS):
        for part in range(3):
            for c in range(DIL_WIDTH // LANE):
                base = o_dil + (3 * g + part) * DIL_WIDTH + c * LANE
                dil_cols.append(base + (perm if part < 2 else np.arange(LANE)))
    main_cols = np.concatenate([np.arange(o_kpe), np.where(pad >= 0, o_kpe + pad, -1)] + dil_cols)
    out['w_in_main'] = _take_cols(w_in, main_cols).astype(BF16)
    out['w_gates'] = w_in[:, o_gate:].astype(BF16)

    head_cols = np.concatenate([np.arange(QK_NOPE), np.where(pad >= 0, QK_NOPE + pad, -1)])
    uq_cols = np.concatenate([np.where(head_cols >= 0, h * QK_HEAD + head_cols, -1) for h in range(MLA_HEADS)])
    out['w_uq'] = _take_cols(w['w_uq'], uq_cols).astype(BF16)
    ukv = w['w_ukv'].reshape(KV_LORA, MLA_HEADS, QK_NOPE + V_HEAD)
    out['w_uk'] = ukv[:, :, :QK_NOPE].reshape(KV_LORA, -1).astype(BF16)
    out['w_uv'] = ukv[:, :, QK_NOPE:].reshape(KV_LORA, -1).astype(BF16)
    out['mla_q_gain'] = _take_cols(w['mla_q_norm'], head_cols).reshape(1, -1)
    out['mla_k_gain'] = _take_cols(w['mla_k_norm'], head_cols).reshape(1, -1)
    out['dil_q_gain'] = _take_cols(w['dil_q_norm'], perm % DIL_HEAD)
    out['dil_k_gain'] = _take_cols(w['dil_k_norm'], perm % DIL_HEAD)
    head_of_lane = np.concatenate([perm // DIL_HEAD, 2 + perm // DIL_HEAD])
    out['dil_seg'] = jnp.asarray(head_of_lane[:, None] == head_of_lane[None, :], BF16)
    return out


def _rope_tables(seq):
    pos = jnp.arange(seq, dtype=F32)[:, None]

    def angles(r):
        inv = ROPE_THETA ** (-jnp.arange(0, r, 2, dtype=F32) / r)
        ang = pos * inv[None, :]
        return jnp.cos(ang), jnp.sin(ang)

    cos, sin = angles(QK_ROPE)
    half = QK_ROPE // 2
    zeros = jnp.zeros((seq, LANE // 2 - half), F32)
    cm = jnp.concatenate([cos, zeros, cos, zeros], axis=1)
    sm = jnp.concatenate([-sin, zeros, sin, zeros], axis=1)

    cos, sin = angles(DIL_ROT)
    rest = LANE // 2 - DIL_ROT
    cd = jnp.concatenate([cos, cos, jnp.ones((seq, rest), F32)] * 2, axis=1)
    sd = jnp.concatenate([-sin, -sin, jnp.zeros((seq, rest), F32), sin, sin, jnp.zeros((seq, rest), F32)], axis=1)
    return cm, sm, cd, sd


def _encoder_layer(x, p, w):
    batch, seq, _ = x.shape
    x = x.reshape(batch * seq, D_MODEL)
    p = p.reshape(batch * seq, PLE_DIM)
    x1 = _ffn(x, w, 'ffn1')
    q, k, v, *dil = _inproj(x1, w, _rope_tables(seq), seq)
    o_mla = _mla(q, k, v, batch, seq)
    dil_outs = [_dilated_group(dil[g], g) for g in range(N_GROUPS)]
    x2 = _merge(x1, o_mla, dil_outs, w, seq)
    return _ffn(x2, w, 'ffn2', p).reshape(batch, seq, D_MODEL)


def kernel(x_prompt, x_sample, p_prompt, p_sample, ffn1_norm, ffn1_w_gate, ffn1_w_up, ffn1_w_down, mix_norm, w_in, q_a_norm, w_uq, kv_a_norm, w_ukv, mla_q_norm, mla_k_norm, dil_q_norm, dil_k_norm, w_o_mla, w_o_dil, w_out, ffn2_norm, ffn2_w_gate, ffn2_w_up, ffn2_w_down, ple_norm, w_ple, w_ple_gate):
    stacked = dict(ffn1_norm=ffn1_norm, ffn1_w_gate=ffn1_w_gate, ffn1_w_up=ffn1_w_up, ffn1_w_down=ffn1_w_down,
                   mix_norm=mix_norm, w_in=w_in, q_a_norm=q_a_norm, w_uq=w_uq, kv_a_norm=kv_a_norm, w_ukv=w_ukv,
                   mla_q_norm=mla_q_norm, mla_k_norm=mla_k_norm, dil_q_norm=dil_q_norm, dil_k_norm=dil_k_norm,
                   w_o_mla=w_o_mla, w_o_dil=w_o_dil, w_out=w_out, ffn2_norm=ffn2_norm, ffn2_w_gate=ffn2_w_gate,
                   ffn2_w_up=ffn2_w_up, ffn2_w_down=ffn2_w_down, ple_norm=ple_norm, w_ple=w_ple,
                   w_ple_gate=w_ple_gate)
    depth = ffn1_norm.shape[0]
    layers = [_prep_weights({name: a[i] for name, a in stacked.items()}) for i in range(depth)]

    def run(x, p):
        for i in range(depth):
            x = _encoder_layer(x, p[i], layers[i])
        return x

    return run(x_prompt, p_prompt), run(x_sample, p_sample)
```

```python
import functools

import numpy as np
import jax
import jax.numpy as jnp
from jax import lax
from jax.experimental import pallas as pl
from jax.experimental.pallas import tpu as pltpu

F32 = jnp.float32
BF16 = jnp.bfloat16

D_MODEL = 1024
PLE_DIM = 256
D_FF = 2816
ROPE_THETA = 500000.0
NORM_EPS = 1e-6
NEG_INF = -1e30
MLA_HEADS = 8
Q_LORA = 512
KV_LORA = 256
QK_NOPE = 128
QK_ROPE = 64
QK_HEAD = QK_NOPE + QK_ROPE
V_HEAD = 128
DIL_PATTERN = ((128, 1), (512, 4), (2048, 16))
N_GROUPS = 3
DIL_HEADS = 8
DIL_HEAD = 64
DIL_ROT = DIL_HEAD // 4
DIL_WIDTH = DIL_HEADS * DIL_HEAD
DIL_QKV = 3 * N_GROUPS * DIL_WIDTH
N_SIDE = 64
assert all(w // (2 * d) == N_SIDE for w, d in DIL_PATTERN)

LOG2_E = 1.4426950408889634
LANE = 128
HEAD_PAD = 2 * LANE
IN_HEAD = Q_LORA + KV_LORA + LANE
VMEM_LIMIT = 56 * 1024 * 1024

TM_FFN = 512
TM_PROJ = 256
TM_MERGE = 256
TQ_MLA = 1024
TK_MLA = 1024
TL_DIL = 512
DIL_SUB = 128
FF_CHUNK = D_FF // 2
MLA_UNROLL = 4
EPI_ROWS = 64


def _cparams(*sem):
    return pltpu.CompilerParams(dimension_semantics=sem, vmem_limit_bytes=VMEM_LIMIT)


def _const_spec(shape):
    nd = len(shape)
    return pl.BlockSpec(shape, lambda *_: (0,) * nd, pipeline_mode=pl.Buffered(1))


def _rms(x, g):
    return x * lax.rsqrt(jnp.mean(x * x, axis=-1, keepdims=True) + NORM_EPS) * g


def _dot(a, b):
    return jnp.dot(a, b, preferred_element_type=F32)


def _dot_nt(a, b):
    return lax.dot_general(a, b, (((1,), (1,)), ((), ())), preferred_element_type=F32)


def _ffn_body(x, g_ref, wg_ref, wu_ref, wd_ref):
    h = _rms(x, g_ref[...]).astype(BF16)
    acc = jnp.zeros_like(x)
    for c in range(D_FF // FF_CHUNK):
        cols = slice(c * FF_CHUNK, (c + 1) * FF_CHUNK)
        gate = _dot(h, wg_ref[:, cols])
        up = _dot(h, wu_ref[:, cols])
        act = (gate * jax.nn.sigmoid(gate) * up).astype(BF16)
        acc = acc + _dot(act, wd_ref[cols, :])
    return x + 0.5 * acc


def _ffn_kernel(x_ref, g_ref, wg_ref, wu_ref, wd_ref, o_ref):
    o_ref[...] = _ffn_body(x_ref[...], g_ref, wg_ref, wu_ref, wd_ref)


def _ffn_ple_kernel(x_ref, p_ref, g_ref, wg_ref, wu_ref, wd_ref, gp_ref, wpg_ref, wpe_ref, o_ref):
    y = _ffn_body(x_ref[...], g_ref, wg_ref, wu_ref, wd_ref)
    gate = jax.nn.sigmoid(_dot(_rms(y, gp_ref[...]).astype(BF16), wpg_ref[...]))
    o_ref[...] = y + gate * _dot(p_ref[...].astype(BF16), wpe_ref[...])


def _ffn(x, w, prefix, p=None):
    t = x.shape[0]
    tm = min(TM_FFN, t)
    row = pl.BlockSpec((tm, D_MODEL), lambda i: (i, 0))
    ffn_w = [w[prefix + '_norm'], w[prefix + '_w_gate'], w[prefix + '_w_up'], w[prefix + '_w_down']]
    if p is None:
        kern, args, specs = _ffn_kernel, [x] + ffn_w, [row]
    else:
        kern, args = _ffn_ple_kernel, [x, p] + ffn_w + [w['ple_norm'], w['w_ple_gate'], w['w_ple']]
        specs = [row, pl.BlockSpec((tm, PLE_DIM), lambda i: (i, 0))]
    specs = specs + [_const_spec(a.shape) for a in args[len(specs):]]
    return pl.pallas_call(
        kern, grid=(t // tm,), in_specs=specs, out_specs=row,
        out_shape=jax.ShapeDtypeStruct((t, D_MODEL), F32),
        compiler_params=_cparams("parallel"), name=prefix)(*args)


def _rope64(y, cos, sin):
    return y * cos + pltpu.roll(y, 64, 1) * sin


def _first_head_qk_lanes(rows):
    lane = lax.broadcasted_iota(jnp.int32, (rows, LANE), 1)
    return jnp.where(lane < 8, 1, jnp.where(lane < 16, 0, jnp.where(lane < 72, 1, 0))) > 0


def _store_regrouped(dil_ref, stage_ref, c):
    dilation, rows, _ = dil_ref.shape
    cols = slice(c * LANE, (c + 1) * LANE)
    if dilation == 1:
        dil_ref[0, :, cols] = stage_ref[...].astype(BF16)
        return
    for r in range(dilation):
        dil_ref[r, :, cols] = stage_ref[pl.ds(r, rows, stride=dilation), :].astype(BF16)


RAW_Q = 0
RAW_K = RAW_Q + 2 * MLA_HEADS
RAW_KPE = RAW_K + MLA_HEADS
RAW_DIL = RAW_KPE + 1
RAW_CHUNKS = RAW_DIL + DIL_QKV // LANE


def _inproj_kernel(x_ref, gmix_ref, win_ref, gqa_ref, wuq_ref, gkva_ref, wuk_ref, wuv_ref,
                   gq_ref, gk_ref, gdq_ref, gdk_ref, cm_ref, sm_ref, cd_ref, sd_ref,
                   seg_ref, q_ref, k_ref, v_ref, dil0_ref, dil1_ref, dil2_ref, raw_ref, rawv_ref):
    i = pl.program_id(0)
    cur = lax.rem(i, 2)
    prev = 1 - cur
    tm = x_ref.shape[0]

    @pl.when(i == 0)
    def _():
        raw_ref[1] = jnp.zeros(raw_ref.shape[1:], F32)
        rawv_ref[1] = jnp.zeros(rawv_ref.shape[1:], BF16)

    def put(c0, val):
        for c in range(val.shape[1] // LANE):
            raw_ref[cur, c0 + c] = val[:, c * LANE:(c + 1) * LANE]

    u = _rms(x_ref[...], gmix_ref[...]).astype(BF16)
    group_cols = 3 * DIL_WIDTH
    chunks = DIL_WIDTH // LANE

    def project_mla():
        head = _dot(u, win_ref[:, :IN_HEAD])
        c_q = _rms(head[:, :Q_LORA], gqa_ref[...]).astype(BF16)
        c_kv = _rms(head[:, Q_LORA:Q_LORA + KV_LORA], gkva_ref[...]).astype(BF16)
        put(RAW_KPE, head[:, Q_LORA + KV_LORA:])
        put(RAW_Q, _dot(c_q, wuq_ref[...]))
        put(RAW_K, _dot(c_kv, wuk_ref[...]))
        rawv_ref[cur] = _dot(c_kv, wuv_ref[...]).astype(BF16)

    def project_group(g):
        base = IN_HEAD + g * group_cols
        sec = _dot(u, win_ref[:, base:base + group_cols])
        for j, scale in enumerate((DIL_HEAD ** -0.5 * LOG2_E, 1.0)):
            for c in range(0, chunks, 2):
                x2 = sec[:, (j * chunks + c) * LANE:(j * chunks + c + 2) * LANE]
                ss = _dot((x2 * x2).astype(BF16), seg_ref[...])
                put(RAW_DIL + g * 3 * chunks + j * chunks + c,
                    x2 * (lax.rsqrt(ss * (1.0 / DIL_HEAD) + NORM_EPS) * scale))
        put(RAW_DIL + g * 3 * chunks + 2 * chunks, sec[:, 2 * DIL_WIDTH:])

    v_ref[...] = rawv_ref[prev]
    gq, gk = gq_ref[...], gk_ref[...]
    q_scale = QK_HEAD ** -0.5 * LOG2_E
    dil_refs = (dil0_ref, dil1_ref, dil2_ref)

    def finish_rows(r0):
        rows = pl.ds(r0, EPI_ROWS)
        get = lambda c: raw_ref[prev, c, rows, :]
        cm, sm = cm_ref[rows, :], sm_ref[rows, :]
        k_pe = get(RAW_KPE)
        k_rot = _rope64(k_pe * gk[:, LANE:], cm, sm)
        k_pe_ss = jnp.sum(k_pe * k_pe, axis=-1, keepdims=True)
        for h in range(MLA_HEADS):
            lo, mid, hi = h * HEAD_PAD, h * HEAD_PAD + LANE, (h + 1) * HEAD_PAD
            q0, q1 = get(RAW_Q + 2 * h), get(RAW_Q + 2 * h + 1)
            ss = jnp.sum(q0 * q0 + q1 * q1, axis=-1, keepdims=True)
            r = lax.rsqrt(ss * (1.0 / QK_HEAD) + NORM_EPS) * q_scale
            q_ref[rows, lo:mid] = (q0 * r * gq[:, :LANE]).astype(BF16)
            q_ref[rows, mid:hi] = (_rope64(q1 * gq[:, LANE:], cm, sm) * r).astype(BF16)
            kn = get(RAW_K + h)
            ssk = jnp.sum(kn * kn, axis=-1, keepdims=True) + k_pe_ss
            rk = lax.rsqrt(ssk * (1.0 / QK_HEAD) + NORM_EPS)
            k_ref[rows, lo:mid] = (kn * rk * gk[:, :LANE]).astype(BF16)
            k_ref[rows, mid:hi] = (k_rot * rk).astype(BF16)

        cd, sd = cd_ref[rows, :], sd_ref[rows, :]
        for g, dil_ref in enumerate(dil_refs):
            for j, gain_ref in enumerate((gdq_ref, gdk_ref)):
                gain = gain_ref[g:g + 1, :]
                for c in range(j * chunks, (j + 1) * chunks):
                    cid = RAW_DIL + g * 3 * chunks + c
                    out = _rope64(get(cid) * gain, cd, sd)
                    if dil_ref.shape[0] == 1:
                        dil_ref[0, rows, c * LANE:(c + 1) * LANE] = out.astype(BF16)
                    else:
                        raw_ref[prev, cid, rows, :] = out

    matmul_parts = [project_mla] + [functools.partial(project_group, g) for g in range(N_GROUPS)]
    epilogue_parts = [functools.partial(finish_rows, r0) for r0 in range(0, tm, EPI_ROWS)]
    for n in range(max(len(matmul_parts), len(epilogue_parts))):
        for parts in (matmul_parts, epilogue_parts):
            if n < len(parts):
                parts[n]()

    for g, dil_ref in enumerate(dil_refs):
        first = RAW_DIL + g * 3 * chunks
        done = 2 * chunks if dil_ref.shape[0] == 1 else 0
        for c in range(done, 3 * chunks):
            _store_regrouped(dil_ref, raw_ref.at[prev, first + c], c)


def _inproj(x1, w, tabs, seq):
    t = x1.shape[0]
    tm = min(TM_PROJ, seq)
    nper = seq // tm
    ntile = t // tm
    lag = lambda i: jnp.maximum(i - 1, 0)
    row = lambda n: pl.BlockSpec((tm, n), lambda i: (lag(i), 0))
    tab = pl.BlockSpec((tm, LANE), lambda i: (lag(i) % nper, 0))
    consts = [w['mix_norm'], w['w_in_main'], w['q_a_norm'], w['w_uq'], w['kv_a_norm'], w['w_uk'], w['w_uv'],
              w['mla_q_gain'], w['mla_k_gain'], w['dil_q_gain'], w['dil_k_gain']]
    widths = (MLA_HEADS * HEAD_PAD, MLA_HEADS * HEAD_PAD, MLA_HEADS * V_HEAD)
    group_cols = 3 * DIL_WIDTH
    dil_specs = [pl.BlockSpec((None, d, tm // d, group_cols), lambda i: (lag(i) // nper, 0, lag(i) % nper, 0))
                 for _, d in DIL_PATTERN]
    dil_shapes = [jax.ShapeDtypeStruct((t // seq, d, seq // d, group_cols), BF16) for _, d in DIL_PATTERN]
    return pl.pallas_call(
        _inproj_kernel, grid=(ntile + 1,),
        in_specs=[pl.BlockSpec((tm, D_MODEL), lambda i: (jnp.minimum(i, ntile - 1), 0))]
                 + [_const_spec(a.shape) for a in consts] + [tab] * 4 + [_const_spec(w['dil_seg'].shape)],
        out_specs=[row(n) for n in widths] + dil_specs,
        out_shape=[jax.ShapeDtypeStruct((t, n), BF16) for n in widths] + dil_shapes,
        scratch_shapes=[pltpu.VMEM((2, RAW_CHUNKS, tm, LANE), F32), pltpu.VMEM((2, tm, MLA_HEADS * V_HEAD), BF16)],
        compiler_params=_cparams("arbitrary"), name="inproj")(x1, *consts, *tabs, w['dil_seg'])


def _mla_kernel(q_ref, k_ref, v_ref, o_ref, kt_ref, s_ref, *, tk, nk):
    tq = q_ref.shape[0]
    ones = jnp.ones((tk, LANE), BF16)

    def chunk(j):
        return pl.ds(j * tk if isinstance(j, int) else pl.multiple_of(j * tk, tk), tk)

    @pl.when(pl.program_id(2) == 0)
    def _():
        def transpose(j, _):
            kt_ref[:, chunk(j)] = k_ref[chunk(j), :].T
            return 0
        lax.fori_loop(0, nk, transpose, 0)

    def scores(j):
        return _dot(q_ref[...], kt_ref[:, chunk(j)])

    def update(j, s, m, acc):
        m_new = jnp.maximum(m, jnp.max(s, axis=-1, keepdims=True))
        p = jnp.exp2(s - m_new).astype(BF16)
        v_ext = jnp.concatenate([v_ref[chunk(j), :], ones], axis=1)
        return m_new, jnp.exp2(m - m_new) * acc + _dot(p, v_ext)

    m = jnp.full((tq, 1), -jnp.inf, F32)
    acc = jnp.zeros((tq, V_HEAD + LANE), F32)
    unroll = min(MLA_UNROLL, nk)
    assert nk % unroll == 0 and unroll % 2 == 0
    s_ref[0] = scores(0)

    def body(jj, carry):
        m, acc = carry
        for u in range(unroll):
            j = unroll * jj + u
            if u + 1 < unroll:
                s_ref[(u + 1) % 2] = scores(j + 1)
            elif nk > unroll:
                s_ref[0] = scores(jnp.minimum(j + 1, nk - 1))
            m, acc = update(j, s_ref[u % 2], m, acc)
        return m, acc

    if nk == unroll:
        m, acc = body(0, (m, acc))
    else:
        m, acc = lax.fori_loop(0, nk // unroll, body, (m, acc))
    o_ref[...] = (acc[:, :V_HEAD] / acc[:, V_HEAD:]).astype(BF16)


def _mla(q, k, v, batch, seq):
    t = q.shape[0]
    tq, tk = min(TQ_MLA, seq), min(TK_MLA, seq)
    nq = seq // tq
    return pl.pallas_call(
        functools.partial(_mla_kernel, tk=tk, nk=seq // tk),
        grid=(batch, MLA_HEADS, nq),
        in_specs=[pl.BlockSpec((tq, HEAD_PAD), lambda b, h, i: (b * nq + i, h)),
                  pl.BlockSpec((seq, HEAD_PAD), lambda b, h, i: (b, h)),
                  pl.BlockSpec((seq, V_HEAD), lambda b, h, i: (b, h))],
        out_specs=pl.BlockSpec((tq, V_HEAD), lambda b, h, i: (b * nq + i, h)),
        out_shape=jax.ShapeDtypeStruct((t, MLA_HEADS * V_HEAD), BF16),
        scratch_shapes=[pltpu.VMEM((HEAD_PAD, seq), BF16), pltpu.VMEM((2, tq, tk), F32)],
        compiler_params=_cparams("parallel", "parallel", "arbitrary"), name="mla")(q, k, v)


def _dil_kernel(q_ref, kc_ref, vc_ref, *rest, tl, sub_len):
    *halo_refs, o_ref, lse_ref = rest
    if halo_refs:
        kp_ref, kn_ref, vp_ref, vn_ref = halo_refs
        sb = min(DIL_SUB, tl)
        halo = N_SIDE
        first_pos = pl.program_id(1) * tl
        k = jnp.concatenate([kp_ref[...], kc_ref[...], kn_ref[...]], axis=0)
        v = jnp.concatenate([vp_ref[...], vc_ref[...], vn_ref[...]], axis=0)
    else:
        sb = min(tl, max(sub_len, 2 * LANE))
        halo = 0
        k, v = kc_ref[...], vc_ref[...]
    wk = sb + 2 * halo
    in_a = _first_head_qk_lanes(sb)
    first = lax.broadcasted_iota(jnp.int32, (sb, LANE), 1) < DIL_HEAD
    zero = jnp.zeros((), BF16)
    ones = jnp.ones((wk, LANE), BF16)
    out_rows, lse_rows = [], []
    for r0 in range(0, tl, sb):
        q_pos = first_pos + r0 if halo_refs else 0
        q_idx = q_pos + lax.broadcasted_iota(jnp.int32, (sb, 1), 0)
        k_idx = q_pos - halo + lax.broadcasted_iota(jnp.int32, (1, wk), 1)
        seq_start = 0 if halo_refs else q_idx // sub_len * sub_len
        lo = jnp.maximum(q_idx - N_SIDE, seq_start)
        hi = jnp.minimum(q_idx + N_SIDE, seq_start + sub_len - 1)
        bias = jnp.where(k_idx >= lo, jnp.where(k_idx <= hi, 0.0, NEG_INF), NEG_INF)
        bias2 = jnp.concatenate([bias, bias], axis=0)
        outs, lses = [], []
        for c in range(DIL_WIDTH // LANE):
            cols = slice(c * LANE, (c + 1) * LANE)
            qp, kpair = q_ref[r0:r0 + sb, cols], k[r0:r0 + wk, cols]
            v_ext = jnp.concatenate([v[r0:r0 + wk, cols], ones], axis=1)
            q2 = jnp.concatenate([jnp.where(in_a, qp, zero), jnp.where(in_a, zero, qp)], axis=0)
            s = _dot_nt(q2, kpair) + bias2
            m = jnp.max(s, axis=-1, keepdims=True)
            pv = _dot(jnp.exp2(s - m).astype(BF16), v_ext)
            l = pv[:, LANE:]
            o2 = pv[:, :LANE] / l
            lse2 = m + jnp.log2(l)
            outs.append(jnp.where(first, o2[:sb], o2[sb:]))
            lses.append(jnp.where(first, lse2[:sb], lse2[sb:]))
        out_rows.append(jnp.concatenate(outs, axis=1))
        lse_rows.append(jnp.concatenate(lses, axis=1))
    o_ref[...] = jnp.concatenate(out_rows, axis=0)
    lse_ref[...] = jnp.concatenate(lse_rows, axis=0)


def _dilated_group(dil, g):
    batch, dilation, sub_len, group_cols = dil.shape
    n_sub = batch * dilation
    total = n_sub * sub_len
    rows2d = dil.reshape(total, group_cols)

    def spec(rows, part, blk, per_sub):
        return pl.BlockSpec((rows, DIL_WIDTH), lambda s, i: (s * per_sub + blk(i), part))

    cur = lambda i: i
    if sub_len <= DIL_SUB:
        tl = min(TL_DIL, total)
        grid = (total // tl, 1)
        specs = [spec(tl, part, cur, 1) for part in range(3)]
    else:
        tl = min(TL_DIL, sub_len)
        nblk = sub_len // tl
        grid = (n_sub, nblk)
        halo_per_blk = tl // N_SIDE
        n_halo = sub_len // N_SIDE
        prev = lambda i: jnp.maximum(i * halo_per_blk - 1, 0)
        nxt = lambda i: jnp.minimum((i + 1) * halo_per_blk, n_halo - 1)
        specs = [spec(tl, part, cur, nblk) for part in range(3)]
        specs += [spec(N_SIDE, part, blk, n_halo) for part in (1, 2) for blk in (prev, nxt)]
    out_spec = specs[0]
    out_sds = jax.ShapeDtypeStruct((total, DIL_WIDTH), F32)
    o, lse = pl.pallas_call(
        functools.partial(_dil_kernel, tl=tl, sub_len=sub_len),
        grid=grid, in_specs=specs, out_specs=[out_spec, out_spec], out_shape=[out_sds, out_sds],
        compiler_params=_cparams("parallel", "arbitrary"), name=f"dilated{g}")(*([rows2d] * len(specs)))
    shape4d = (batch, dilation, sub_len, DIL_WIDTH)
    return o.reshape(shape4d), lse.reshape(shape4d)


def _token_order(blk_ref, stage_ref):
    dilation, rows, n = blk_ref.shape
    if dilation == 1:
        return blk_ref[0]
    for c in range(n // LANE):
        for r in range(dilation):
            stage_ref[c, pl.ds(r, rows, stride=dilation), :] = blk_ref[r, :, c * LANE:(c + 1) * LANE]
    return jnp.concatenate([stage_ref[c] for c in range(n // LANE)], axis=1)


def _merge_kernel(x_ref, om_ref, o0_ref, o1_ref, o2_ref, l0_ref, l1_ref, l2_ref,
                  gmix_ref, wgate_ref, woa_ref, wob_ref, wout_ref, y_ref, *stage_refs):
    x = x_ref[...]
    outs = [_token_order(o0_ref, None), _token_order(o1_ref, stage_refs[0]), _token_order(o2_ref, stage_refs[1])]
    lses = [_token_order(l0_ref, None), _token_order(l1_ref, stage_refs[2]), _token_order(l2_ref, stage_refs[3])]
    top = jnp.maximum(jnp.maximum(lses[0], lses[1]), lses[2])
    e = [jnp.exp2(l - top) for l in lses]
    o_dil = (e[0] * outs[0] + e[1] * outs[1] + e[2] * outs[2]) / (e[0] + e[1] + e[2])
    u = _rms(x, gmix_ref[...]).astype(BF16)
    gates = jax.nn.sigmoid(_dot(u, wgate_ref[...]))
    a = _dot(om_ref[...], woa_ref[...])
    b = _dot(o_dil.astype(BF16), wob_ref[...])
    mix = gates[:, :D_MODEL] * a + gates[:, D_MODEL:] * b
    y_ref[...] = x + _dot(mix.astype(BF16), wout_ref[...])


def _merge(x1, o_mla, dil_outs, w, seq):
    t = x1.shape[0]
    tm = min(TM_MERGE, seq)
    nper = seq // tm
    row = lambda n: pl.BlockSpec((tm, n), lambda i: (i, 0))
    outs = [o for o, _ in dil_outs]
    lses = [l for _, l in dil_outs]
    sub_specs = [pl.BlockSpec((None, d, tm // d, DIL_WIDTH), lambda i: (i // nper, 0, i % nper, 0))
                 for _, d in DIL_PATTERN]
    consts = [w['mix_norm'], w['w_gates'], w['w_o_mla'], w['w_o_dil'], w['w_out']]
    return pl.pallas_call(
        _merge_kernel, grid=(t // tm,),
        in_specs=[row(D_MODEL), row(MLA_HEADS * V_HEAD)] + sub_specs * 2
                 + [_const_spec(a.shape) for a in consts],
        out_specs=row(D_MODEL), out_shape=jax.ShapeDtypeStruct((t, D_MODEL), F32),
        scratch_shapes=[pltpu.VMEM((DIL_WIDTH // LANE, tm, LANE), F32)] * 4,
        compiler_params=_cparams("parallel"), name="merge")(x1, o_mla, *outs, *lses, *consts)


def _pair_perm():
    r = DIL_ROT // 2
    a, b = np.arange(DIL_HEAD), DIL_HEAD + np.arange(DIL_HEAD)
    return np.concatenate([a[:r], b[:r], a[2 * r:], a[r:2 * r], b[r:2 * r], b[2 * r:]])


def _rope_pad_cols():
    half = QK_ROPE // 2
    cols = -np.ones(LANE, np.int64)
    cols[:half] = np.arange(half)
    cols[LANE // 2:LANE // 2 + half] = half + np.arange(half)
    return cols


def _take_cols(a, cols):
    cols = np.asarray(cols)
    out = jnp.take(a, jnp.asarray(np.maximum(cols, 0)), axis=-1)
    return jnp.where(jnp.asarray(cols >= 0), out, 0)


def _prep_weights(w):
    pad = _rope_pad_cols()
    perm = _pair_perm()
    out = dict(w)
    for name in ('ffn1_norm', 'mix_norm', 'q_a_norm', 'kv_a_norm', 'ffn2_norm', 'ple_norm'):
        out[name] = w[name].reshape(1, -1)
    for name in ('ffn1_w_gate', 'ffn1_w_up', 'ffn1_w_down', 'ffn2_w_gate', 'ffn2_w_up', 'ffn2_w_down',
                 'w_o_mla', 'w_o_dil', 'w_out', 'w_ple', 'w_ple_gate'):
        out[name] = w[name].astype(BF16)

    w_in = w['w_in']
    o_kpe, o_dil, o_gate = Q_LORA + KV_LORA, Q_LORA + KV_LORA + QK_ROPE, Q_LORA + KV_LORA + QK_ROPE + DIL_QKV
    dil_cols = []
    for g in range(N_GROUPS):
        for part in range(3):
            for c in range(DIL_WIDTH // LANE):
                base = o_dil + (3 * g + part) * DIL_WIDTH + c * LANE
                dil_cols.append(base + (perm if part < 2 else np.arange(LANE)))
    main_cols = np.concatenate([np.arange(o_kpe), np.where(pad >= 0, o_kpe + pad, -1)] + dil_cols)
    out['w_in_main'] = _take_cols(w_in, main_cols).astype(BF16)
    out['w_gates'] = w_in[:, o_gate:].astype(BF16)

    head_cols = np.concatenate([np.arange(QK_NOPE), np.where(pad >= 0, QK_NOPE + pad, -1)])
    uq_cols = np.concatenate([np.where(head_cols >= 0, h * QK_HEAD + head_cols, -1) for h in range(MLA_HEADS)])
    out['w_uq'] = _take_cols(w['w_uq'], uq_cols).astype(BF16)
    ukv = w['w_ukv'].reshape(KV_LORA, MLA_HEADS, QK_NOPE + V_HEAD)
    out['w_uk'] = ukv[:, :, :QK_NOPE].reshape(KV_LORA, -1).astype(BF16)
    out['w_uv'] = ukv[:, :, QK_NOPE:].reshape(KV_LORA, -1).astype(BF16)
    out['mla_q_gain'] = _take_cols(w['mla_q_norm'], head_cols).reshape(1, -1)
    out['mla_k_gain'] = _take_cols(w['mla_k_norm'], head_cols).reshape(1, -1)
    out['dil_q_gain'] = _take_cols(w['dil_q_norm'], perm % DIL_HEAD)
    out['dil_k_gain'] = _take_cols(w['dil_k_norm'], perm % DIL_HEAD)
    head_of_lane = np.concatenate([perm // DIL_HEAD, 2 + perm // DIL_HEAD])
    out['dil_seg'] = jnp.asarray(head_of_lane[:, None] == head_of_lane[None, :], BF16)
    return out


def _rope_tables(seq):
    pos = jnp.arange(seq, dtype=F32)[:, None]

    def angles(r):
        inv = ROPE_THETA ** (-jnp.arange(0, r, 2, dtype=F32) / r)
        ang = pos * inv[None, :]
        return jnp.cos(ang), jnp.sin(ang)

    cos, sin = angles(QK_ROPE)
    half = QK_ROPE // 2
    zeros = jnp.zeros((seq, LANE // 2 - half), F32)
    cm = jnp.concatenate([cos, zeros, cos, zeros], axis=1)
    sm = jnp.concatenate([-sin, zeros, sin, zeros], axis=1)

    cos, sin = angles(DIL_ROT)
    rest = LANE // 2 - DIL_ROT
    cd = jnp.concatenate([cos, cos, jnp.ones((seq, rest), F32)] * 2, axis=1)
    sd = jnp.concatenate([-sin, -sin, jnp.zeros((seq, rest), F32), sin, sin, jnp.zeros((seq, rest), F32)], axis=1)
    return cm, sm, cd, sd


def _encoder_layer(x, p, w):
    batch, seq, _ = x.shape
    x = x.reshape(batch * seq, D_MODEL)
    p = p.reshape(batch * seq, PLE_DIM)
    x1 = _ffn(x, w, 'ffn1')
    q, k, v, *dil = _inproj(x1, w, _rope_tables(seq), seq)
    o_mla = _mla(q, k, v, batch, seq)
    dil_outs = [_dilated_group(dil[g], g) for g in range(N_GROUPS)]
    x2 = _merge(x1, o_mla, dil_outs, w, seq)
    return _ffn(x2, w, 'ffn2', p).reshape(batch, seq, D_MODEL)


def kernel(x_prompt, x_sample, p_prompt, p_sample, ffn1_norm, ffn1_w_gate, ffn1_w_up, ffn1_w_down, mix_norm, w_in, q_a_norm, w_uq, kv_a_norm, w_ukv, mla_q_norm, mla_k_norm, dil_q_norm, dil_k_norm, w_o_mla, w_o_dil, w_out, ffn2_norm, ffn2_w_gate, ffn2_w_up, ffn2_w_down, ple_norm, w_ple, w_ple_gate):
    stacked = dict(ffn1_norm=ffn1_norm, ffn1_w_gate=ffn1_w_gate, ffn1_w_up=ffn1_w_up, ffn1_w_down=ffn1_w_down,
                   mix_norm=mix_norm, w_in=w_in, q_a_norm=q_a_norm, w_uq=w_uq, kv_a_norm=kv_a_norm, w_ukv=w_ukv,
                   mla_q_norm=mla_q_norm, mla_k_norm=mla_k_norm, dil_q_norm=dil_q_norm, dil_k_norm=dil_k_norm,
                   w_o_mla=w_o_mla, w_o_dil=w_o_dil, w_out=w_out, ffn2_norm=ffn2_norm, ffn2_w_gate=ffn2_w_gate,
                   ffn2_w_up=ffn2_w_up, ffn2_w_down=ffn2_w_down, ple_norm=ple_norm, w_ple=w_ple,
                   w_ple_gate=w_ple_gate)
    depth = ffn1_norm.shape[0]
    layers = [_prep_weights({name: a[i] for name, a in stacked.items()}) for i in range(depth)]

    def run(x, p):
        for i in range(depth):
            x = _encoder_layer(x, p[i], layers[i])
        return x

    return run(x_prompt, p_prompt), run(x_sample, p_sample)
```
